```python
import math
import jax, jax.numpy as jnp
from jax import lax
import numpy as np

D_MODEL = 1024
BATCH = 2
SEQ = 16384
DEPTH = 2

D_FF = 2816
DN_ALPHA = (2.0 * DEPTH) ** 0.25
DN_BETA = (8.0 * DEPTH) ** -0.25
LN_EPS = 1e-5
N_MOD = 9
RG_WIDTH = D_MODEL
RG_BLOCKS = 16
RG_BLOCK = RG_WIDTH // RG_BLOCKS
CONV_W = 4
RG_C = 8.0
GLA_HEADS = 4
GLA_DK = (D_MODEL // 2) // GLA_HEADS
GLA_DV = D_MODEL // GLA_HEADS
GLA_RANK = 16
GLA_TAU = 16.0
GLA_CHUNK = 64
EVEN_SPLITS = (RG_WIDTH, RG_WIDTH, GLA_HEADS * GLA_DK, GLA_HEADS * GLA_DK,
               GLA_HEADS * GLA_DV, GLA_HEADS * GLA_DV, GLA_RANK)
EVEN_OFFSETS = tuple(int(o) for o in np.cumsum(EVEN_SPLITS)[:-1])
EVEN_IN = sum(EVEN_SPLITS)
EVEN_MIX = RG_WIDTH + GLA_HEADS * GLA_DV
MLA_HEADS = 8
MLA_NOPE = 128
MLA_ROPE = 64
MLA_V = 128
MLA_Q_RANK = 256
MLA_KV_RANK = 128
MLA_QBLOCK = 128
ROPE_THETA = 10000.0
ODD_IN = MLA_Q_RANK + MLA_KV_RANK + MLA_ROPE
N_EVEN = (DEPTH + 1) // 2
N_ODD = DEPTH // 2

kernel_name = "hybrid_rglru_gla_mla_macaron_deepnorm_adaln"

F32 = jnp.float32


def layer_norm(x, g, b):
    xf = x.astype(F32)
    mu = jnp.mean(xf, -1, keepdims=True)
    var = jnp.mean(jnp.square(xf - mu), -1, keepdims=True)
    return ((xf - mu) * lax.rsqrt(var + LN_EPS) * g + b).astype(x.dtype)


def rms_norm(x, g):
    xf = x.astype(F32)
    return (xf * lax.rsqrt(jnp.mean(xf * xf, -1, keepdims=True) + LN_EPS) * g).astype(x.dtype)


def modulate(x, shift, scale):
    return x * (1 + scale[:, None, :]) + shift[:, None, :]


def deepnorm_residual(x, f, gate, g, b, weight=1.0):
    return layer_norm(DN_ALPHA * x + weight * (1 + gate[:, None, :]) * f, g, b)


def swiglu(u, w_in, w_out):
    gate, up = jnp.split(u @ w_in, 2, axis=-1)
    return (jax.nn.silu(gate) * up) @ w_out


def causal_depthwise_conv(x, w, b):
    C = x.shape[-1]
    y = lax.conv_general_dilated(x, w[:, None, :], window_strides=(1,),
                                 padding=[(CONV_W - 1, 0)],
                                 dimension_numbers=('NWC', 'WIO', 'NWC'),
                                 feature_group_count=C)
    return y + b


def rg_lru(x, w_a, b_a, w_x, b_x, lam):
    B_, S, C = x.shape
    xb = x.reshape(B_, S, RG_BLOCKS, RG_BLOCK)
    r = jax.nn.sigmoid(jnp.einsum('bsni,nij->bsnj', xb, w_a).reshape(B_, S, C) + b_a)
    i = jax.nn.sigmoid(jnp.einsum('bsni,nij->bsnj', xb, w_x).reshape(B_, S, C) + b_x)
    log_a = -RG_C * r.astype(F32) * jax.nn.softplus(-lam.astype(F32))
    a = jnp.exp(log_a)
    u = jnp.sqrt(-jnp.expm1(2.0 * log_a)) * (i * x).astype(F32)

    def combine(left, right):
        a1, b1 = left
        a2, b2 = right
        return a1 * a2, a2 * b1 + b2

    _, h = lax.associative_scan(combine, (a, u), axis=1)
    return h.astype(x.dtype)


def gla_chunked(q, k, v, log_alpha):
    B_, S, H, DK = q.shape
    DV = v.shape[-1]
    C = GLA_CHUNK
    N = S // C

    def chunk(t):
        return t.astype(F32).reshape(B_, N, C, H, t.shape[-1]).transpose(0, 1, 3, 2, 4)

    qc, kc, vc = chunk(q), chunk(k), chunk(v)
    bc = jnp.cumsum(chunk(log_alpha), axis=3)
    b_last = bc[..., -1:, :]
    q_dec = qc * jnp.exp(bc)
    k_inv = kc * jnp.exp(-bc)
    k_end = kc * jnp.exp(b_last - bc)
    mask = jnp.tril(jnp.ones((C, C), dtype=bool))
    scores = jnp.where(mask, jnp.einsum('bnhid,bnhjd->bnhij', q_dec, k_inv), 0.0)
    o_intra = jnp.einsum('bnhij,bnhjv->bnhiv', scores, vc)

    def step(state, inp):
        q_n, k_n, v_n, dec_n = inp
        o = jnp.einsum('bhcd,bhdv->bhcv', q_n, state)
        state = dec_n[..., None] * state + jnp.einsum('bhcd,bhcv->bhdv', k_n, v_n)
        return state, o

    s0 = jnp.zeros((B_, H, DK, DV), F32)
    xs = (q_dec.swapaxes(0, 1), k_end.swapaxes(0, 1), vc.swapaxes(0, 1),
          jnp.exp(b_last[..., 0, :]).swapaxes(0, 1))
    _, o_inter = lax.scan(step, s0, xs)
    o = o_intra + o_inter.swapaxes(0, 1)
    return o.transpose(0, 1, 3, 2, 4).reshape(B_, S, H, DV)


def even_mixer(u, w_in, conv_w, conv_b, rg_wa, rg_ba, rg_wx, rg_bx, rg_lam,
               gla_wa2, gla_ba, gla_norm_g, w_out):
    B_, S, _ = u.shape
    xa, ga, q, k, v, g, za = jnp.split(u @ w_in, EVEN_OFFSETS, axis=-1)
    xa = causal_depthwise_conv(xa, conv_w, conv_b)
    ya = rg_lru(xa, rg_wa, rg_ba, rg_wx, rg_bx, rg_lam) * jax.nn.gelu(ga)
    log_alpha = jax.nn.log_sigmoid((za @ gla_wa2 + gla_ba).astype(F32)) / GLA_TAU
    o = gla_chunked(q.reshape(B_, S, GLA_HEADS, GLA_DK) * (GLA_DK ** -0.5),
                    k.reshape(B_, S, GLA_HEADS, GLA_DK),
                    v.reshape(B_, S, GLA_HEADS, GLA_DV),
                    log_alpha.reshape(B_, S, GLA_HEADS, GLA_DK))
    yb = rms_norm(o, gla_norm_g).reshape(B_, S, GLA_HEADS * GLA_DV).astype(u.dtype) * jax.nn.silu(g)
    return jnp.concatenate([ya, yb], axis=-1) @ w_out


def rope_tables(positions):
    half = MLA_ROPE // 2
    freqs = ROPE_THETA ** (-jnp.arange(half, dtype=F32) / half)
    ang = positions.astype(F32)[..., None] * freqs
    return jnp.cos(ang)[:, :, None, :], jnp.sin(ang)[:, :, None, :]


def apply_rope(x, cos, sin):
    half = MLA_ROPE // 2
    x1, x2 = x[..., :half].astype(F32), x[..., half:].astype(F32)
    return jnp.concatenate([x1 * cos - x2 * sin, x2 * cos + x1 * sin], -1).astype(x.dtype)


def causal_block_attention(q, k, v):
    B_, S, H, Dqk = q.shape
    scale = Dqk ** -0.5
    nblk = S // MLA_QBLOCK
    qb = q.reshape(B_, nblk, MLA_QBLOCK, H, Dqk).swapaxes(0, 1)
    key_idx = jnp.arange(S)

    def one_block(args):
        q_blk, blk = args
        s = jnp.einsum('bqhd,bkhd->bhqk', q_blk, k).astype(F32) * scale
        q_idx = blk * MLA_QBLOCK + jnp.arange(MLA_QBLOCK)
        s = jnp.where(key_idx[None, :] <= q_idx[:, None], s, -jnp.inf)
        p = jax.nn.softmax(s, axis=-1).astype(v.dtype)
        return jnp.einsum('bhqk,bkhd->bqhd', p, v)

    o = lax.map(one_block, (qb, jnp.arange(nblk)))
    return o.swapaxes(0, 1).reshape(B_, S, H, v.shape[-1])


def odd_mixer(u, positions, w_in, q_norm_g, w_q_up, kv_norm_g, w_kv_up, w_out):
    B_, S, _ = u.shape
    cq, ckv, k_pe = jnp.split(u @ w_in, [MLA_Q_RANK, MLA_Q_RANK + MLA_KV_RANK], axis=-1)
    q = (rms_norm(cq, q_norm_g) @ w_q_up).reshape(B_, S, MLA_HEADS, MLA_NOPE + MLA_ROPE)
    kv = (rms_norm(ckv, kv_norm_g) @ w_kv_up).reshape(B_, S, MLA_HEADS, MLA_NOPE + MLA_V)
    q_nope, q_pe = q[..., :MLA_NOPE], q[..., MLA_NOPE:]
    k_nope, v = kv[..., :MLA_NOPE], kv[..., MLA_NOPE:]
    cos, sin = rope_tables(positions)
    q_pe = apply_rope(q_pe, cos, sin)
    k_pe = apply_rope(k_pe[:, :, None, :], cos, sin)
    q = jnp.concatenate([q_nope, q_pe], -1)
    k = jnp.concatenate([k_nope, jnp.broadcast_to(k_pe, (B_, S, MLA_HEADS, MLA_ROPE))], -1)
    o = causal_block_attention(q, k, v)
    return o.reshape(B_, S, MLA_HEADS * MLA_V) @ w_out


def setup_inputs(seed: int = 0) -> dict:
    key = jax.random.key(seed)
    ks = iter(jax.random.split(key, 40))
    nrm = lambda shape, s: jax.random.normal(next(ks), shape, F32) * s
    x = jax.random.normal(next(ks), (BATCH, SEQ, D_MODEL), F32)
    c = jax.random.normal(next(ks), (BATCH, D_MODEL), F32)
    offs = jax.random.randint(next(ks), (BATCH, 1), 0, 1024, dtype=jnp.int32)
    positions = jnp.arange(SEQ, dtype=jnp.int32)[None, :] + offs
    mod_w = nrm((DEPTH, D_MODEL, N_MOD * D_MODEL), 0.1 * D_MODEL ** -0.5)
    mod_b = nrm((DEPTH, N_MOD * D_MODEL), 0.01)
    ffn_in = nrm((DEPTH, 2, D_MODEL, 2 * D_FF), D_MODEL ** -0.5)
    ffn_out = nrm((DEPTH, 2, D_FF, D_MODEL), DN_BETA * D_FF ** -0.5)
    ln_g = 1.0 + nrm((DEPTH, 3, D_MODEL), 0.02)
    ln_b = nrm((DEPTH, 3, D_MODEL), 0.02)
    even_w_in = nrm((N_EVEN, D_MODEL, EVEN_IN), D_MODEL ** -0.5)
    conv_w = nrm((N_EVEN, CONV_W, RG_WIDTH), CONV_W ** -0.5)
    conv_b = nrm((N_EVEN, RG_WIDTH), 0.01)
    rg_wa = nrm((N_EVEN, RG_BLOCKS, RG_BLOCK, RG_BLOCK), RG_BLOCK ** -0.5)
    rg_ba = nrm((N_EVEN, RG_WIDTH), 0.01)
    rg_wx = nrm((N_EVEN, RG_BLOCKS, RG_BLOCK, RG_BLOCK), RG_BLOCK ** -0.5)
    rg_bx = nrm((N_EVEN, RG_WIDTH), 0.01)
    a_c = jax.random.uniform(next(ks), (N_EVEN, RG_WIDTH), F32, 0.9, 0.999)
    a0 = a_c ** (1.0 / RG_C)
    rg_lam = jnp.log(a0) - jnp.log1p(-a0)
    gla_wa2 = nrm((N_EVEN, GLA_RANK, GLA_HEADS * GLA_DK), GLA_RANK ** -0.5)
    gla_ba = nrm((N_EVEN, GLA_HEADS * GLA_DK), 0.1)
    gla_norm_g = 1.0 + nrm((N_EVEN, GLA_DV), 0.02)
    even_w_out = nrm((N_EVEN, EVEN_MIX, D_MODEL), DN_BETA * EVEN_MIX ** -0.5)
    odd_w_in = nrm((N_ODD, D_MODEL, ODD_IN), D_MODEL ** -0.5)
    q_norm_g = 1.0 + nrm((N_ODD, MLA_Q_RANK), 0.02)
    w_q_up = nrm((N_ODD, MLA_Q_RANK, MLA_HEADS * (MLA_NOPE + MLA_ROPE)), MLA_Q_RANK ** -0.5)
    kv_norm_g = 1.0 + nrm((N_ODD, MLA_KV_RANK), 0.02)
    w_kv_up = nrm((N_ODD, MLA_KV_RANK, MLA_HEADS * (MLA_NOPE + MLA_V)), MLA_KV_RANK ** -0.5)
    odd_w_out = nrm((N_ODD, MLA_HEADS * MLA_V, D_MODEL), DN_BETA * (MLA_HEADS * MLA_V) ** -0.5)
    return {"x": x, "c": c, "positions": positions, "mod_w": mod_w, "mod_b": mod_b,
            "ffn_in": ffn_in, "ffn_out": ffn_out, "ln_g": ln_g, "ln_b": ln_b,
            "even_w_in": even_w_in, "conv_w": conv_w, "conv_b": conv_b,
            "rg_wa": rg_wa, "rg_ba": rg_ba, "rg_wx": rg_wx, "rg_bx": rg_bx, "rg_lam": rg_lam,
            "gla_wa2": gla_wa2, "gla_ba": gla_ba, "gla_norm_g": gla_norm_g,
            "even_w_out": even_w_out, "odd_w_in": odd_w_in, "q_norm_g": q_norm_g,
            "w_q_up": w_q_up, "kv_norm_g": kv_norm_g, "w_kv_up": w_kv_up,
            "odd_w_out": odd_w_out}


def reference(x, c, positions, mod_w, mod_b, ffn_in, ffn_out, ln_g, ln_b,
              even_w_in, conv_w, conv_b, rg_wa, rg_ba, rg_wx, rg_bx, rg_lam,
              gla_wa2, gla_ba, gla_norm_g, even_w_out, odd_w_in, q_norm_g,
              w_q_up, kv_norm_g, w_kv_up, odd_w_out):
    c_act = jax.nn.silu(c)
    for l in range(DEPTH):
        mod = c_act @ mod_w[l] + mod_b[l]
        sh1, sc1, g1, sh2, sc2, g2, sh3, sc3, g3 = jnp.split(mod, N_MOD, axis=-1)
        f = swiglu(modulate(x, sh1, sc1), ffn_in[l, 0], ffn_out[l, 0])
        x = deepnorm_residual(x, f, g1, ln_g[l, 0], ln_b[l, 0], 0.5)
        u = modulate(x, sh2, sc2)
        if l % 2 == 0:
            e = l // 2
            m = even_mixer(u, even_w_in[e], conv_w[e], conv_b[e], rg_wa[e], rg_ba[e],
                           rg_wx[e], rg_bx[e], rg_lam[e], gla_wa2[e], gla_ba[e],
                           gla_norm_g[e], even_w_out[e])
        else:
            o = l // 2
            m = odd_mixer(u, positions, odd_w_in[o], q_norm_g[o], w_q_up[o],
                          kv_norm_g[o], w_kv_up[o], odd_w_out[o])
        x = deepnorm_residual(x, m, g2, ln_g[l, 1], ln_b[l, 1])
        f = swiglu(modulate(x, sh3, sc3), ffn_in[l, 1], ffn_out[l, 1])
        x = deepnorm_residual(x, f, g3, ln_g[l, 2], ln_b[l, 2], 0.5)
    return x
```

```python
import functools
import math

import jax
import jax.numpy as jnp
from jax import lax
from jax.experimental import pallas as pl
from jax.experimental.pallas import tpu as pltpu

F32 = jnp.float32
BF16 = jnp.bfloat16

LN_EPS = 1e-5
RG_BLOCKS = 16
RG_C = 8.0
CONV_W = 4
GLA_HEADS = 4
GLA_DK = 128
GLA_DV = 256
GLA_RANK = 16
GLA_TAU = 16.0
GLA_CHUNK = 64
MLA_HEADS = 8
MLA_NOPE = 128
MLA_ROPE = 64
MLA_V = 128
MLA_Q_RANK = 256
MLA_KV_RANK = 128
ROPE_THETA = 10000.0
N_MOD = 9

LANES = 128
SUBLANES = 8
VMEM_CAP = 56 << 20

TM_FFN = 512
TM_PROJ = 512
TM_EVEN_IN = 256
TM_ODD_IN = 256
TC_RGLRU = 512
TC_GLA = 256
TQ_ATTN = 512
FF_CHUNK = 256


def _params(semantics, vmem_bytes):
    return pltpu.CompilerParams(dimension_semantics=semantics,
                                vmem_limit_bytes=int(min(max(vmem_bytes, 16 << 20), VMEM_CAP)))


def _const_spec(shape):
    nd = len(shape)
    return pl.BlockSpec(shape, lambda *_: (0,) * nd, pipeline_mode=pl.Buffered(1))


def _dot(a, b):
    return jnp.dot(a, b, preferred_element_type=F32)


def _dot_nt(a, b):
    return lax.dot_general(a, b, (((1,), (1,)), ((), ())), preferred_element_type=F32)


def _dot_tn(a, b):
    return lax.dot_general(a, b, (((0,), (0,)), ((), ())), preferred_element_type=F32)


def _sigmoid(x):
    return 1.0 / (1.0 + jnp.exp(-x))


def _silu(x):
    return x * _sigmoid(x)


def _softplus(x):
    return jnp.maximum(x, 0.0) + jnp.log1p(jnp.exp(-jnp.abs(x)))


def _log_sigmoid(x):
    return jnp.minimum(x, 0.0) - jnp.log1p(jnp.exp(-jnp.abs(x)))


def _gelu_tanh(x):
    c = math.sqrt(2.0 / math.pi)
    return x * (0.5 * (1.0 + jnp.tanh(c * (x + 0.044715 * (x * x * x)))))


def _layer_norm(y, g, b):
    mu = jnp.mean(y, axis=-1, keepdims=True)
    d = y - mu
    var = jnp.mean(d * d, axis=-1, keepdims=True)
    return d * lax.rsqrt(var + LN_EPS) * g + b


def _rms_norm(y, g):
    return y * lax.rsqrt(jnp.mean(y * y, axis=-1, keepdims=True) + LN_EPS) * g


def _modulated(x, mod_ref, row):
    return x * (1.0 + mod_ref[row + 1:row + 2, :]) + mod_ref[row:row + 1, :]


def _mod_kernel(c_ref, w_ref, b_ref, o_ref):
    ca = _silu(c_ref[...]).astype(BF16)
    o_ref[...] = _dot(ca, w_ref[...].astype(BF16)) + b_ref[...]


def _mod_vectors(c, mod_w, mod_b):
    depth, d, n = mod_w.shape
    bsz = c.shape[0]
    tn = d
    out = pl.pallas_call(
        _mod_kernel,
        grid=(depth, n // tn),
        in_specs=[pl.BlockSpec((bsz, d), lambda l, j: (0, 0)),
                  pl.BlockSpec((None, d, tn), lambda l, j: (l, 0, j)),
                  pl.BlockSpec((None, 1, tn), lambda l, j: (l, 0, j))],
        out_specs=pl.BlockSpec((None, bsz, tn), lambda l, j: (l, 0, j)),
        out_shape=jax.ShapeDtypeStruct((depth, bsz, n), F32),
        compiler_params=_params(("arbitrary", "arbitrary"), 4 * d * tn * 4),
        name="adaln_mod",
    )(c, mod_w, mod_b.reshape(depth, 1, n))
    return out.reshape(depth, bsz, N_MOD, d)


def _ffn_kernel(x_ref, mod_ref, wg_ref, wu_ref, wo_ref, lng_ref, lnb_ref, o_ref, act_ref,
                *, row, alpha, chunk):
    x = x_ref[...]
    u = _modulated(x, mod_ref, row).astype(BF16)
    d_ff = wg_ref.shape[1]
    for j in range(d_ff // chunk):
        sl = slice(j * chunk, (j + 1) * chunk)
        g = _dot(u, wg_ref[:, sl])
        p = _dot(u, wu_ref[:, sl])
        act_ref[:, sl] = (_silu(g) * p).astype(BF16)
    f = _dot(act_ref[...], wo_ref[...])
    gate = mod_ref[row + 2:row + 3, :]
    y = alpha * x + (0.5 * (1.0 + gate)) * f
    o_ref[...] = _layer_norm(y, lng_ref[...], lnb_ref[...])


def _ffn_sublayer(x, mod_l, row, w_in, w_out, ln_g, ln_b, alpha, seq):
    t, d = x.shape
    d_ff = w_out.shape[0]
    tm = min(TM_FFN, seq)
    tpb = seq // tm
    wg = w_in[:, :d_ff].astype(BF16)
    wu = w_in[:, d_ff:].astype(BF16)
    wo = w_out.astype(BF16)
    vmem = 3 * d * d_ff * 2 + 4 * tm * d * 4 + tm * d_ff * 2 + 6 * tm * d * 4 + (8 << 20)
    return pl.pallas_call(
        functools.partial(_ffn_kernel, row=row, alpha=alpha, chunk=min(FF_CHUNK, d_ff)),
        grid=(t // tm,),
        in_specs=[pl.BlockSpec((tm, d), lambda r: (r, 0)),
                  pl.BlockSpec((None, N_MOD, d), lambda r: (r // tpb, 0, 0)),
                  _const_spec((d, d_ff)), _const_spec((d, d_ff)), _const_spec((d_ff, d)),
                  _const_spec((1, d)), _const_spec((1, d))],
        out_specs=pl.BlockSpec((tm, d), lambda r: (r, 0)),
        out_shape=jax.ShapeDtypeStruct((t, d), F32),
        scratch_shapes=[pltpu.VMEM((tm, d_ff), BF16)],
        compiler_params=_params(("arbitrary",), vmem),
        name="ffn_sublayer",
    )(x, mod_l, wg, wu, wo, ln_g.reshape(1, d), ln_b.reshape(1, d))


def _proj_dn_kernel(*refs, n_in, row, alpha):
    x_ref, mod_ref = refs[0], refs[1]
    a_refs = refs[2:2 + n_in]
    w_refs = refs[2 + n_in:2 + 2 * n_in]
    lng_ref, lnb_ref, o_ref = refs[2 + 2 * n_in:]
    m = _dot(a_refs[0][...], w_refs[0][...])
    for a_ref, w_ref in zip(a_refs[1:], w_refs[1:]):
        m = m + _dot(a_ref[...], w_ref[...])
    gate = mod_ref[row + 2:row + 3, :]
    y = alpha * x_ref[...] + (1.0 + gate) * m
    o_ref[...] = _layer_norm(y, lng_ref[...], lnb_ref[...])


def _proj_deepnorm(x, mod_l, row, acts, weights, ln_g, ln_b, alpha, seq):
    t, d = x.shape
    tm = min(TM_PROJ, seq)
    tpb = seq // tm
    n_in = len(acts)
    row_spec = lambda w: pl.BlockSpec((tm, w), lambda r: (r, 0))
    vmem = sum(2 * w.size * 2 for w in weights) + sum(2 * tm * a.shape[1] * 2 for a in acts) + 8 * tm * d * 4 + (4 << 20)
    return pl.pallas_call(
        functools.partial(_proj_dn_kernel, n_in=n_in, row=row, alpha=alpha),
        grid=(t // tm,),
        in_specs=([row_spec(d), pl.BlockSpec((None, N_MOD, d), lambda r: (r // tpb, 0, 0))]
                  + [row_spec(a.shape[1]) for a in acts]
                  + [_const_spec(w.shape) for w in weights]
                  + [_const_spec((1, d)), _const_spec((1, d))]),
        out_specs=row_spec(d),
        out_shape=jax.ShapeDtypeStruct((t, d), F32),
        compiler_params=_params(("arbitrary",), vmem),
        name="proj_deepnorm",
    )(x, mod_l, *acts, *weights, ln_g.reshape(1, d), ln_b.reshape(1, d))


def _even_in_kernel(x_ref, mod_ref, wxa, wga, wq, wk, wv, wg, wza, wa2, ba,
                    xa_o, gg_o, q_o, k_o, v_o, gs_o, la_o, *, row):
    u = _modulated(x_ref[...], mod_ref, row).astype(BF16)
    xa_o[...] = _dot(u, wxa[...])
    gg_o[...] = _gelu_tanh(_dot(u, wga[...]))
    q_o[...] = _dot(u, wq[...]) * (GLA_DK ** -0.5)
    k_o[...] = _dot(u, wk[...])
    v_o[...] = _dot(u, wv[...]).astype(BF16)
    gs_o[...] = _silu(_dot(u, wg[...]))
    za = _dot(u, wza[...]).astype(BF16)
    z = _dot(za, wa2[...]) + ba[...]
    la_o[...] = _log_sigmoid(z) * (1.0 / GLA_TAU)


def _even_in(x, mod_l, row, w_in, gla_wa2, gla_ba, seq):
    t, d = x.shape
    tm = min(TM_EVEN_IN, seq)
    tpb = seq // tm
    hk = GLA_HEADS * GLA_DK
    hv = GLA_HEADS * GLA_DV
    offs = [0, d, 2 * d, 2 * d + hk, 2 * d + 2 * hk, 2 * d + 2 * hk + hv, 2 * d + 2 * hk + 2 * hv]
    wb = w_in.astype(BF16)
    wxa, wga, wq, wk, wv, wg = [wb[:, offs[i]:offs[i + 1]] for i in range(6)]
    wza = jnp.pad(wb[:, offs[6]:], ((0, 0), (0, LANES - GLA_RANK)))
    wa2 = jnp.pad(gla_wa2.astype(BF16), ((0, LANES - GLA_RANK), (0, 0)))
    ws = [wxa, wga, wq, wk, wv, wg, wza, wa2]
    widths = [d, d, hk, hk, hv, hv, hk]
    dtypes = [F32, F32, F32, F32, BF16, F32, F32]
    vmem = sum(w.size * 2 for w in ws) + 2 * tm * sum(widths) * 4 + 4 * tm * d * 4 + (8 << 20)
    return pl.pallas_call(
        functools.partial(_even_in_kernel, row=row),
        grid=(t // tm,),
        in_specs=([pl.BlockSpec((tm, d), lambda r: (r, 0)),
                   pl.BlockSpec((None, N_MOD, d), lambda r: (r // tpb, 0, 0))]
                  + [_const_spec(w.shape) for w in ws] + [_const_spec((1, hk))]),
        out_specs=[pl.BlockSpec((tm, w), lambda r: (r, 0)) for w in widths],
        out_shape=[jax.ShapeDtypeStruct((t, w), dt) for w, dt in zip(widths, dtypes)],
        compiler_params=_params(("arbitrary",), vmem),
        name="even_in_proj",
    )(x, mod_l, *ws, gla_ba.reshape(1, hk))


def _rglru_kernel(xa_ref, gg_ref, cw_ref, cb_ref, wa_ref, wx_ref, ba_ref, bx_ref, lam_ref,
                  o_ref, xpad_ref, a_ref, u_ref, carry_ref, *, tc, groups):
    first = pl.program_id(1) == 0

    @pl.when(first)
    def _():
        xpad_ref[0:SUBLANES, :] = jnp.zeros((SUBLANES, xpad_ref.shape[1]), F32)
        carry_ref[...] = jnp.zeros(carry_ref.shape, F32)

    x = xa_ref[...]
    xpad_ref[SUBLANES:, :] = x
    xc = cw_ref[CONV_W - 1:CONV_W, :] * x + cb_ref[...]
    for dlt in range(1, CONV_W):
        xc = xc + cw_ref[CONV_W - 1 - dlt:CONV_W - dlt, :] * xpad_ref[pl.ds(SUBLANES - dlt, tc), :]
    xpad_ref[0:SUBLANES, :] = x[tc - SUBLANES:, :]

    xb = xc.astype(BF16)
    gw = xb.shape[1] // groups
    neg_c_sp = -RG_C * _softplus(-lam_ref[...])
    for g in range(groups):
        sl = slice(g * gw, (g + 1) * gw)
        r = _sigmoid(_dot(xb[:, sl], wa_ref[g]) + ba_ref[:, sl])
        i = _sigmoid(_dot(xb[:, sl], wx_ref[g]) + bx_ref[:, sl])
        log_a = neg_c_sp[:, sl] * r
        a = jnp.exp(log_a)
        a_ref[:, sl] = a
        u_ref[:, sl] = jnp.sqrt(-jnp.tanh(log_a) * (a * a + 1.0)) * (i * xc[:, sl])

    width = a_ref.shape[1]
    rows = lax.broadcasted_iota(jnp.int32, (SUBLANES, width), 0)

    def body(n, carry):
        r0 = pl.multiple_of(n * SUBLANES, SUBLANES)
        a = a_ref[pl.ds(r0, SUBLANES), :]
        b = u_ref[pl.ds(r0, SUBLANES), :]
        for dlt in (1, 2, 4):
            keep = rows >= dlt
            a_sh = jnp.where(keep, pltpu.roll(a, dlt, 0), 1.0)
            b_sh = jnp.where(keep, pltpu.roll(b, dlt, 0), 0.0)
            b = a * b_sh + b
            a = a * a_sh
        h = b + a * carry
        o_ref[pl.ds(r0, SUBLANES), :] = (h * gg_ref[pl.ds(r0, SUBLANES), :]).astype(o_ref.dtype)
        return jnp.broadcast_to(h[SUBLANES - 1:SUBLANES, :], h.shape)

    carry_ref[...] = lax.fori_loop(0, tc // SUBLANES, body, carry_ref[...], unroll=2)


def _rglru(xa, gg, conv_w, conv_b, rg_wa, rg_ba, rg_wx, rg_bx, rg_lam, seq):
    t, c = xa.shape
    tc = min(TC_RGLRU, seq)
    nt = seq // tc
    bs = c // RG_BLOCKS
    per = (2 * LANES) // bs
    groups = RG_BLOCKS // per

    def block_diag(w):
        w4 = w.reshape(groups, per, bs, bs)
        eye = jnp.eye(per, dtype=w.dtype)
        return jnp.einsum("gaij,ac->gaicj", w4, eye).reshape(groups, per * bs, per * bs).astype(BF16)

    vec = lambda v: v.reshape(1, c)
    tile = pl.BlockSpec((tc, c), lambda b, n: (b * nt + n, 0))
    vmem = 4 * tc * c * 4 + 2 * tc * c * 2 + 3 * tc * c * 4 + 8 * tc * c * 4 + (4 << 20)
    return pl.pallas_call(
        functools.partial(_rglru_kernel, tc=tc, groups=groups),
        grid=(t // seq, nt),
        in_specs=[tile, tile, _const_spec((CONV_W, c)), _const_spec((1, c)),
                  _const_spec((groups, per * bs, per * bs)), _const_spec((groups, per * bs, per * bs)),
                  _const_spec((1, c)), _const_spec((1, c)), _const_spec((1, c))],
        out_specs=tile,
        out_shape=jax.ShapeDtypeStruct((t, c), BF16),
        scratch_shapes=[pltpu.VMEM((tc + SUBLANES, c), F32), pltpu.VMEM((tc, c), F32),
                        pltpu.VMEM((tc, c), F32), pltpu.VMEM((SUBLANES, c), F32)],
        compiler_params=_params(("arbitrary", "arbitrary"), vmem),
        name="rglru",
    )(xa, gg, conv_w, vec(conv_b), block_diag(rg_wa), block_diag(rg_wx), vec(rg_ba), vec(rg_bx), vec(rg_lam))


def _gla_kernel(q_ref, k_ref, la_ref, v_ref, gs_ref, ng_ref, o_ref, st_ref, *, tc):
    @pl.when(pl.program_id(1) == 0)
    def _():
        st_ref[...] = jnp.zeros(st_ref.shape, F32)

    ch = GLA_CHUNK
    ri = lax.broadcasted_iota(jnp.int32, (ch, ch), 0)
    ci = lax.broadcasted_iota(jnp.int32, (ch, ch), 1)
    causal = ri >= ci
    tri = jnp.where(causal, 1.0, 0.0).astype(BF16)
    ng = ng_ref[...]
    for c in range(tc // ch):
        rs = slice(c * ch, (c + 1) * ch)
        for h in range(GLA_HEADS):
            ks = slice(h * GLA_DK, (h + 1) * GLA_DK)
            vs = slice(h * GLA_DV, (h + 1) * GLA_DV)
            la = la_ref[rs, ks]
            hi = la.astype(BF16)
            r1 = la - hi.astype(F32)
            mid = r1.astype(BF16)
            lo = (r1 - mid.astype(F32)).astype(BF16)
            bc = _dot(tri, hi) + _dot(tri, mid) + _dot(tri, lo)
            b_last = bc[ch - 1:ch, :]
            e_pos = jnp.exp(bc)
            e_neg = jnp.exp(-bc)
            kk = k_ref[rs, ks]
            qd = (q_ref[rs, ks] * e_pos).astype(BF16)
            ki = (kk * e_neg).astype(BF16)
            ke = (kk * jnp.exp(b_last - bc)).astype(BF16)
            vc = v_ref[rs, vs]
            st = st_ref[h]
            s = jnp.where(causal, _dot_nt(qd, ki), 0.0).astype(BF16)
            o = _dot(s, vc) + _dot_nt(qd, st.astype(BF16))
            st_ref[h] = st * jnp.exp(b_last) + _dot_tn(vc, ke)
            o_ref[rs, vs] = (_rms_norm(o, ng) * gs_ref[rs, vs]).astype(o_ref.dtype)


def _gla(q, k, la, v, gs, norm_g, seq):
    t = q.shape[0]
    tc = min(TC_GLA, seq)
    nt = seq // tc
    hk = GLA_HEADS * GLA_DK
    hv = GLA_HEADS * GLA_DV
    tile_k = pl.BlockSpec((tc, hk), lambda b, n: (b * nt + n, 0))
    tile_v = pl.BlockSpec((tc, hv), lambda b, n: (b * nt + n, 0))
    vmem = 2 * tc * (3 * hk * 4 + hv * 2 + hv * 4 + hv * 2) + GLA_HEADS * GLA_DV * GLA_DK * 4 + (16 << 20)
    return pl.pallas_call(
        functools.partial(_gla_kernel, tc=tc),
        grid=(t // seq, nt),
        in_specs=[tile_k, tile_k, tile_k, tile_v, tile_v, _const_spec((1, GLA_DV))],
        out_specs=tile_v,
        out_shape=jax.ShapeDtypeStruct((t, hv), BF16),
        scratch_shapes=[pltpu.VMEM((GLA_HEADS, GLA_DV, GLA_DK), F32)],
        compiler_params=_params(("arbitrary", "arbitrary"), vmem),
        name="gla",
    )(q, k, la, v, gs, norm_g.reshape(1, GLA_DV))


def _odd_in_kernel(x_ref, mod_ref, pos_ref, freq_ref, sgn_ref, wcq, wckv, wkp, wkps, qg, kvg,
                   wqn, wqp, wqps, wkn, wv, qn_o, qp_o, kn_o, kp_o, v_o, *, row, scale):
    u = _modulated(x_ref[...], mod_ref, row).astype(BF16)
    cqn = _rms_norm(_dot(u, wcq[...]), qg[...]).astype(BF16)
    ckvn = _rms_norm(_dot(u, wckv[...]), kvg[...]).astype(BF16)
    ang = pos_ref[...].astype(F32) * freq_ref[...]
    cos = jnp.cos(ang)
    sin = jnp.sin(ang) * sgn_ref[...]
    kp_o[...] = (_dot(u, wkp[...]) * cos + _dot(u, wkps[...]) * sin).astype(BF16)
    cos_h = jnp.concatenate([cos] * MLA_HEADS, axis=1)
    sin_h = jnp.concatenate([sin] * MLA_HEADS, axis=1)
    qn_o[...] = (_dot(cqn, wqn[...]) * scale).astype(BF16)
    qp_o[...] = ((_dot(cqn, wqp[...]) * cos_h + _dot(cqn, wqps[...]) * sin_h) * scale).astype(BF16)
    kn_o[...] = _dot(ckvn, wkn[...]).astype(BF16)
    v_o[...] = _dot(ckvn, wv[...]).astype(BF16)


def _swap_halves(w):
    half = w.shape[-1] // 2
    return jnp.concatenate([w[..., half:], w[..., :half]], axis=-1)


def _odd_in(x, mod_l, row, positions, w_in, q_norm_g, w_q_up, kv_norm_g, w_kv_up, seq):
    t, d = x.shape
    tm = min(TM_ODD_IN, seq)
    tpb = seq // tm
    hn = MLA_HEADS * MLA_NOPE
    hv = MLA_HEADS * MLA_V
    pad_r = LANES - MLA_ROPE
    wb = w_in.astype(BF16)
    wcq = wb[:, :MLA_Q_RANK]
    wckv = wb[:, MLA_Q_RANK:MLA_Q_RANK + MLA_KV_RANK]
    wkpe = wb[:, MLA_Q_RANK + MLA_KV_RANK:]
    wkp = jnp.pad(wkpe, ((0, 0), (0, pad_r)))
    wkps = jnp.pad(_swap_halves(wkpe), ((0, 0), (0, pad_r)))
    wq = w_q_up.astype(BF16).reshape(MLA_Q_RANK, MLA_HEADS, MLA_NOPE + MLA_ROPE)
    wqn = wq[:, :, :MLA_NOPE].reshape(MLA_Q_RANK, hn)
    wq_pe = wq[:, :, MLA_NOPE:]
    pad3 = ((0, 0), (0, 0), (0, pad_r))
    wqp = jnp.pad(wq_pe, pad3).reshape(MLA_Q_RANK, MLA_HEADS * LANES)
    wqps = jnp.pad(_swap_halves(wq_pe), pad3).reshape(MLA_Q_RANK, MLA_HEADS * LANES)
    wkv = w_kv_up.astype(BF16).reshape(MLA_KV_RANK, MLA_HEADS, MLA_NOPE + MLA_V)
    wkn = wkv[:, :, :MLA_NOPE].reshape(MLA_KV_RANK, hn)
    wv = wkv[:, :, MLA_NOPE:].reshape(MLA_KV_RANK, hv)
    half = MLA_ROPE // 2
    freqs = ROPE_THETA ** (-jnp.arange(half, dtype=F32) / half)
    zeros = jnp.zeros((pad_r,), F32)
    freq_row = jnp.concatenate([freqs, freqs, zeros]).reshape(1, LANES)
    sgn_row = jnp.concatenate([-jnp.ones((half,), F32), jnp.ones((half,), F32), zeros]).reshape(1, LANES)
    ws = [wcq, wckv, wkp, wkps, q_norm_g.reshape(1, -1), kv_norm_g.reshape(1, -1), wqn, wqp, wqps, wkn, wv]
    widths = [hn, MLA_HEADS * LANES, hn, LANES, hv]
    scale = (MLA_NOPE + MLA_ROPE) ** -0.5
    vmem = 2 * sum(w.size * 2 for w in ws) + 2 * tm * sum(widths) * 2 + 16 * tm * d * 4 + (8 << 20)
    return pl.pallas_call(
        functools.partial(_odd_in_kernel, row=row, scale=scale),
        grid=(t // tm,),
        in_specs=([pl.BlockSpec((tm, d), lambda r: (r, 0)),
                   pl.BlockSpec((None, N_MOD, d), lambda r: (r // tpb, 0, 0)),
                   pl.BlockSpec((tm, 1), lambda r: (r, 0)),
                   _const_spec((1, LANES)), _const_spec((1, LANES))]
                  + [_const_spec(w.shape) for w in ws]),
        out_specs=[pl.BlockSpec((tm, w), lambda r: (r, 0)) for w in widths],
        out_shape=[jax.ShapeDtypeStruct((t, w), BF16) for w in widths],
        compiler_params=_params(("arbitrary",), vmem),
        name="odd_in_proj",
    )(x, mod_l, positions.reshape(t, 1), freq_row, sgn_row, *ws)


def _attn_kernel(qn_ref, qp_ref, kn_ref, kp_ref, v_ref, o_ref, m_ref, l_ref, acc_ref, *, tq):
    i = pl.program_id(2)
    q = jnp.concatenate([qn_ref[...], qp_ref[...]], axis=1)
    m_ref[...] = jnp.full(m_ref.shape, -jnp.inf, F32)
    l_ref[...] = jnp.zeros(l_ref.shape, F32)
    acc_ref[...] = jnp.zeros(acc_ref.shape, F32)

    def tile(j, diagonal):
        k0 = pl.multiple_of(j * tq, tq)
        kt = jnp.concatenate([kn_ref[pl.ds(k0, tq), :], kp_ref[pl.ds(k0, tq), :]], axis=1)
        s = _dot_nt(q, kt)
        if diagonal:
            ri = lax.broadcasted_iota(jnp.int32, s.shape, 0)
            ci = lax.broadcasted_iota(jnp.int32, s.shape, 1)
            s = jnp.where(ci <= ri, s, -jnp.inf)
        m_prev = m_ref[...]
        m_new = jnp.maximum(m_prev, jnp.max(s, axis=1, keepdims=True))
        alpha = jnp.exp(m_prev - m_new)
        p = jnp.exp(s - m_new)
        l_ref[...] = alpha * l_ref[...] + jnp.sum(p, axis=1, keepdims=True)
        acc_ref[...] = alpha * acc_ref[...] + _dot(p.astype(BF16), v_ref[pl.ds(k0, tq), :])
        m_ref[...] = m_new

    def body(j, carry):
        tile(j, False)
        return carry

    lax.fori_loop(0, i, body, 0)
    tile(i, True)
    o_ref[...] = (acc_ref[...] / l_ref[...]).astype(o_ref.dtype)


def _attention(qn, qp, kn, kp, v, seq):
    t = qn.shape[0]
    bsz = t // seq
    tq = min(TQ_ATTN, seq)
    nq = seq // tq
    q_spec = pl.BlockSpec((tq, LANES), lambda b, h, i: (b * nq + i, h))
    kv_spec = pl.BlockSpec((seq, LANES), lambda b, h, i: (b, h))
    kp_spec = pl.BlockSpec((seq, LANES), lambda b, h, i: (b, 0))
    vmem = 2 * 3 * seq * LANES * 2 + 6 * tq * LANES * 2 + 8 * tq * tq * 4 + 3 * tq * LANES * 4 + (4 << 20)
    return pl.pallas_call(
        functools.partial(_attn_kernel, tq=tq),
        grid=(bsz, MLA_HEADS, nq),
        in_specs=[q_spec, q_spec, kv_spec, kp_spec, kv_spec],
        out_specs=q_spec,
        out_shape=jax.ShapeDtypeStruct((t, MLA_HEADS * MLA_V), BF16),
        scratch_shapes=[pltpu.VMEM((tq, 1), F32), pltpu.VMEM((tq, 1), F32), pltpu.VMEM((tq, MLA_V), F32)],
        compiler_params=_params(("arbitrary", "arbitrary", "arbitrary"), vmem),
        name="mla_attention",
    )(qn, qp, kn, kp, v)


def kernel(x, c, positions, mod_w, mod_b, ffn_in, ffn_out, ln_g, ln_b, even_w_in, conv_w, conv_b, rg_wa, rg_ba, rg_wx, rg_bx, rg_lam, gla_wa2, gla_ba, gla_norm_g, even_w_out, odd_w_in, q_norm_g, w_q_up, kv_norm_g, w_kv_up, odd_w_out):
    bsz, seq, d = x.shape
    depth = mod_w.shape[0]
    alpha = (2.0 * depth) ** 0.25
    mod = _mod_vectors(c, mod_w, mod_b)
    h = x.reshape(bsz * seq, d)
    for l in range(depth):
        mod_l = mod[l]
        h = _ffn_sublayer(h, mod_l, 0, ffn_in[l, 0], ffn_out[l, 0], ln_g[l, 0], ln_b[l, 0], alpha, seq)
        if l % 2 == 0:
            e = l // 2
            xa, gg, q, k, v, gs, la = _even_in(h, mod_l, 3, even_w_in[e], gla_wa2[e], gla_ba[e], seq)
            ya = _rglru(xa, gg, conv_w[e], conv_b[e], rg_wa[e], rg_ba[e], rg_wx[e], rg_bx[e], rg_lam[e], seq)
            yb = _gla(q, k, la, v, gs, gla_norm_g[e], seq)
            wo = even_w_out[e].astype(BF16)
            rgw = ya.shape[1]
            h = _proj_deepnorm(h, mod_l, 3, [ya, yb], [wo[:rgw], wo[rgw:]], ln_g[l, 1], ln_b[l, 1], alpha, seq)
        else:
            o = l // 2
            qn, qp, kn, kp, v = _odd_in(h, mod_l, 3, positions, odd_w_in[o], q_norm_g[o], w_q_up[o],
                                        kv_norm_g[o], w_kv_up[o], seq)
            att = _attention(qn, qp, kn, kp, v, seq)
            h = _proj_deepnorm(h, mod_l, 3, [att], [odd_w_out[o].astype(BF16)], ln_g[l, 1], ln_b[l, 1], alpha, seq)
        h = _ffn_sublayer(h, mod_l, 6, ffn_in[l, 1], ffn_out[l, 1], ln_g[l, 2], ln_b[l, 2], alpha, seq)
    return h.reshape(bsz, seq, d)
```

```python
import functools
import math

import jax
import jax.numpy as jnp
from jax import lax
from jax.experimental import pallas as pl
from jax.experimental.pallas import tpu as pltpu

F32 = jnp.float32
BF16 = jnp.bfloat16

LN_EPS = 1e-5
RG_BLOCKS = 16
RG_C = 8.0
CONV_W = 4
GLA_HEADS = 4
GLA_DK = 128
GLA_DV = 256
GLA_RANK = 16
GLA_TAU = 16.0
GLA_CHUNK = 64
MLA_HEADS = 8
MLA_NOPE = 128
MLA_ROPE = 64
MLA_V = 128
MLA_Q_RANK = 256
MLA_KV_RANK = 128
ROPE_THETA = 10000.0
N_MOD = 9

LANES = 128
SUBLANES = 8
VMEM_CAP = 56 << 20

TM_FFN = 512
TM_PROJ = 512
TM_EVEN_IN = 256
TM_ODD_IN = 256
TC_RGLRU = 512
TC_GLA = 256
TQ_ATTN = 1024
FF_CHUNK = 256


def _params(semantics, vmem_bytes):
    return pltpu.CompilerParams(dimension_semantics=semantics,
                                vmem_limit_bytes=int(min(max(vmem_bytes, 16 << 20), VMEM_CAP)))


def _const_spec(shape):
    nd = len(shape)
    return pl.BlockSpec(shape, lambda *_: (0,) * nd, pipeline_mode=pl.Buffered(1))


def _dot(a, b):
    return jnp.dot(a, b, preferred_element_type=F32)


def _dot_nt(a, b):
    return lax.dot_general(a, b, (((1,), (1,)), ((), ())), preferred_element_type=F32)


def _dot_tn(a, b):
    return lax.dot_general(a, b, (((0,), (0,)), ((), ())), preferred_element_type=F32)


def _sigmoid(x):
    return 1.0 / (1.0 + jnp.exp(-x))


def _silu(x):
    return x * _sigmoid(x)


def _softplus(x):
    return jnp.maximum(x, 0.0) + jnp.log1p(jnp.exp(-jnp.abs(x)))


def _log_sigmoid(x):
    return jnp.minimum(x, 0.0) - jnp.log1p(jnp.exp(-jnp.abs(x)))


def _gelu_tanh(x):
    c = math.sqrt(2.0 / math.pi)
    return x * (0.5 * (1.0 + jnp.tanh(c * (x + 0.044715 * (x * x * x)))))


def _layer_norm(y, g, b):
    mu = jnp.mean(y, axis=-1, keepdims=True)
    d = y - mu
    var = jnp.mean(d * d, axis=-1, keepdims=True)
    return d * lax.rsqrt(var + LN_EPS) * g + b


def _rms_norm(y, g):
    return y * lax.rsqrt(jnp.mean(y * y, axis=-1, keepdims=True) + LN_EPS) * g


def _modulated(x, mod_ref, row):
    return x * (1.0 + mod_ref[row + 1:row + 2, :]) + mod_ref[row:row + 1, :]


def _mod_kernel(c_ref, w_ref, b_ref, o_ref):
    ca = _silu(c_ref[...]).astype(BF16)
    o_ref[...] = _dot(ca, w_ref[...].astype(BF16)) + b_ref[...]


def _mod_vectors(c, mod_w, mod_b):
    depth, d, n = mod_w.shape
    bsz = c.shape[0]
    tn = d
    out = pl.pallas_call(
        _mod_kernel,
        grid=(depth, n // tn),
        in_specs=[pl.BlockSpec((bsz, d), lambda l, j: (0, 0)),
                  pl.BlockSpec((None, d, tn), lambda l, j: (l, 0, j)),
                  pl.BlockSpec((None, 1, tn), lambda l, j: (l, 0, j))],
        out_specs=pl.BlockSpec((None, bsz, tn), lambda l, j: (l, 0, j)),
        out_shape=jax.ShapeDtypeStruct((depth, bsz, n), F32),
        compiler_params=_params(("arbitrary", "arbitrary"), 4 * d * tn * 4),
        name="adaln_mod",
    )(c, mod_w, mod_b.reshape(depth, 1, n))
    return out.reshape(depth, bsz, N_MOD, d)


def _ffn_kernel(x_ref, mod_ref, wg_ref, wu_ref, wo_ref, lng_ref, lnb_ref, o_ref, act_ref,
                *, row, alpha, chunk):
    x = x_ref[...]
    u = _modulated(x, mod_ref, row).astype(BF16)
    d_ff = wg_ref.shape[1]
    for j in range(d_ff // chunk):
        sl = slice(j * chunk, (j + 1) * chunk)
        g = _dot(u, wg_ref[:, sl])
        p = _dot(u, wu_ref[:, sl])
        act_ref[:, sl] = (_silu(g) * p).astype(BF16)
    f = _dot(act_ref[...], wo_ref[...])
    gate = mod_ref[row + 2:row + 3, :]
    y = alpha * x + (0.5 * (1.0 + gate)) * f
    o_ref[...] = _layer_norm(y, lng_ref[...], lnb_ref[...])


def _ffn_sublayer(x, mod_l, row, w_in, w_out, ln_g, ln_b, alpha, seq):
    t, d = x.shape
    d_ff = w_out.shape[0]
    tm = min(TM_FFN, seq)
    tpb = seq // tm
    wg = w_in[:, :d_ff].astype(BF16)
    wu = w_in[:, d_ff:].astype(BF16)
    wo = w_out.astype(BF16)
    vmem = 3 * d * d_ff * 2 + 4 * tm * d * 4 + tm * d_ff * 2 + 6 * tm * d * 4 + (8 << 20)
    return pl.pallas_call(
        functools.partial(_ffn_kernel, row=row, alpha=alpha, chunk=min(FF_CHUNK, d_ff)),
        grid=(t // tm,),
        in_specs=[pl.BlockSpec((tm, d), lambda r: (r, 0)),
                  pl.BlockSpec((None, N_MOD, d), lambda r: (r // tpb, 0, 0)),
                  _const_spec((d, d_ff)), _const_spec((d, d_ff)), _const_spec((d_ff, d)),
                  _const_spec((1, d)), _const_spec((1, d))],
        out_specs=pl.BlockSpec((tm, d), lambda r: (r, 0)),
        out_shape=jax.ShapeDtypeStruct((t, d), F32),
        scratch_shapes=[pltpu.VMEM((tm, d_ff), BF16)],
        compiler_params=_params(("arbitrary",), vmem),
        name="ffn_sublayer",
    )(x, mod_l, wg, wu, wo, ln_g.reshape(1, d), ln_b.reshape(1, d))


def _proj_dn_kernel(*refs, n_in, row, alpha):
    x_ref, mod_ref = refs[0], refs[1]
    a_refs = refs[2:2 + n_in]
    w_refs = refs[2 + n_in:2 + 2 * n_in]
    lng_ref, lnb_ref, o_ref = refs[2 + 2 * n_in:]
    m = _dot(a_refs[0][...], w_refs[0][...])
    for a_ref, w_ref in zip(a_refs[1:], w_refs[1:]):
        m = m + _dot(a_ref[...], w_ref[...])
    gate = mod_ref[row + 2:row + 3, :]
    y = alpha * x_ref[...] + (1.0 + gate) * m
    o_ref[...] = _layer_norm(y, lng_ref[...], lnb_ref[...])


def _proj_deepnorm(x, mod_l, row, acts, weights, ln_g, ln_b, alpha, seq):
    t, d = x.shape
    tm = min(TM_PROJ, seq)
    tpb = seq // tm
    n_in = len(acts)
    row_spec = lambda w: pl.BlockSpec((tm, w), lambda r: (r, 0))
    vmem = sum(2 * w.size * 2 for w in weights) + sum(2 * tm * a.shape[1] * 2 for a in acts) + 8 * tm * d * 4 + (4 << 20)
    return pl.pallas_call(
        functools.partial(_proj_dn_kernel, n_in=n_in, row=row, alpha=alpha),
        grid=(t // tm,),
        in_specs=([row_spec(d), pl.BlockSpec((None, N_MOD, d), lambda r: (r // tpb, 0, 0))]
                  + [row_spec(a.shape[1]) for a in acts]
                  + [_const_spec(w.shape) for w in weights]
                  + [_const_spec((1, d)), _const_spec((1, d))]),
        out_specs=row_spec(d),
        out_shape=jax.ShapeDtypeStruct((t, d), F32),
        compiler_params=_params(("arbitrary",), vmem),
        name="proj_deepnorm",
    )(x, mod_l, *acts, *weights, ln_g.reshape(1, d), ln_b.reshape(1, d))


def _even_in_kernel(x_ref, mod_ref, wxa, wga, wq, wk, wv, wg, wza, wa2, ba,
                    xa_o, gg_o, q_o, k_o, v_o, gs_o, la_o, *, row):
    u = _modulated(x_ref[...], mod_ref, row).astype(BF16)
    xa_o[...] = _dot(u, wxa[...])
    gg_o[...] = _gelu_tanh(_dot(u, wga[...]))
    q_o[...] = _dot(u, wq[...]) * (GLA_DK ** -0.5)
    k_o[...] = _dot(u, wk[...])
    v_o[...] = _dot(u, wv[...]).astype(BF16)
    gs_o[...] = _silu(_dot(u, wg[...]))
    za = _dot(u, wza[...]).astype(BF16)
    z = _dot(za, wa2[...]) + ba[...]
    la_o[...] = _log_sigmoid(z) * (1.0 / GLA_TAU)


def _even_in(x, mod_l, row, w_in, gla_wa2, gla_ba, seq):
    t, d = x.shape
    tm = min(TM_EVEN_IN, seq)
    tpb = seq // tm
    hk = GLA_HEADS * GLA_DK
    hv = GLA_HEADS * GLA_DV
    offs = [0, d, 2 * d, 2 * d + hk, 2 * d + 2 * hk, 2 * d + 2 * hk + hv, 2 * d + 2 * hk + 2 * hv]
    wb = w_in.astype(BF16)
    wxa, wga, wq, wk, wv, wg = [wb[:, offs[i]:offs[i + 1]] for i in range(6)]
    wza = jnp.pad(wb[:, offs[6]:], ((0, 0), (0, LANES - GLA_RANK)))
    wa2 = jnp.pad(gla_wa2.astype(BF16), ((0, LANES - GLA_RANK), (0, 0)))
    ws = [wxa, wga, wq, wk, wv, wg, wza, wa2]
    widths = [d, d, hk, hk, hv, hv, hk]
    dtypes = [F32, F32, F32, F32, BF16, F32, F32]
    vmem = sum(w.size * 2 for w in ws) + 2 * tm * sum(widths) * 4 + 4 * tm * d * 4 + (8 << 20)
    return pl.pallas_call(
        functools.partial(_even_in_kernel, row=row),
        grid=(t // tm,),
        in_specs=([pl.BlockSpec((tm, d), lambda r: (r, 0)),
                   pl.BlockSpec((None, N_MOD, d), lambda r: (r // tpb, 0, 0))]
                  + [_const_spec(w.shape) for w in ws] + [_const_spec((1, hk))]),
        out_specs=[pl.BlockSpec((tm, w), lambda r: (r, 0)) for w in widths],
        out_shape=[jax.ShapeDtypeStruct((t, w), dt) for w, dt in zip(widths, dtypes)],
        compiler_params=_params(("arbitrary",), vmem),
        name="even_in_proj",
    )(x, mod_l, *ws, gla_ba.reshape(1, hk))


def _rglru_kernel(xa_ref, gg_ref, cw_ref, cb_ref, wa_ref, wx_ref, ba_ref, bx_ref, lam_ref,
                  o_ref, xpad_ref, a_ref, u_ref, carry_ref, *, tc, groups):
    first = pl.program_id(1) == 0

    @pl.when(first)
    def _():
        xpad_ref[0:SUBLANES, :] = jnp.zeros((SUBLANES, xpad_ref.shape[1]), F32)
        carry_ref[...] = jnp.zeros(carry_ref.shape, F32)

    x = xa_ref[...]
    xpad_ref[SUBLANES:, :] = x
    xc = cw_ref[CONV_W - 1:CONV_W, :] * x + cb_ref[...]
    for dlt in range(1, CONV_W):
        xc = xc + cw_ref[CONV_W - 1 - dlt:CONV_W - dlt, :] * xpad_ref[pl.ds(SUBLANES - dlt, tc), :]
    xpad_ref[0:SUBLANES, :] = x[tc - SUBLANES:, :]

    xb = xc.astype(BF16)
    gw = xb.shape[1] // groups
    neg_c_sp = -RG_C * _softplus(-lam_ref[...])
    for g in range(groups):
        sl = slice(g * gw, (g + 1) * gw)
        r = _sigmoid(_dot(xb[:, sl], wa_ref[g]) + ba_ref[:, sl])
        i = _sigmoid(_dot(xb[:, sl], wx_ref[g]) + bx_ref[:, sl])
        log_a = neg_c_sp[:, sl] * r
        a = jnp.exp(log_a)
        a_ref[:, sl] = a
        u_ref[:, sl] = jnp.sqrt(-jnp.tanh(log_a) * (a * a + 1.0)) * (i * xc[:, sl])

    width = a_ref.shape[1]
    rows = lax.broadcasted_iota(jnp.int32, (SUBLANES, width), 0)

    def body(n, carry):
        r0 = pl.multiple_of(n * SUBLANES, SUBLANES)
        a = a_ref[pl.ds(r0, SUBLANES), :]
        b = u_ref[pl.ds(r0, SUBLANES), :]
        for dlt in (1, 2, 4):
            keep = rows >= dlt
            a_sh = jnp.where(keep, pltpu.roll(a, dlt, 0), 1.0)
            b_sh = jnp.where(keep, pltpu.roll(b, dlt, 0), 0.0)
            b = a * b_sh + b
            a = a * a_sh
        h = b + a * carry
        o_ref[pl.ds(r0, SUBLANES), :] = (h * gg_ref[pl.ds(r0, SUBLANES), :]).astype(o_ref.dtype)
        return jnp.broadcast_to(h[SUBLANES - 1:SUBLANES, :], h.shape)

    carry_ref[...] = lax.fori_loop(0, tc // SUBLANES, body, carry_ref[...], unroll=2)


def _rglru(xa, gg, conv_w, conv_b, rg_wa, rg_ba, rg_wx, rg_bx, rg_lam, seq):
    t, c = xa.shape
    tc = min(TC_RGLRU, seq)
    nt = seq // tc
    bs = c // RG_BLOCKS
    per = (2 * LANES) // bs
    groups = RG_BLOCKS // per

    def block_diag(w):
        w4 = w.reshape(groups, per, bs, bs)
        eye = jnp.eye(per, dtype=w.dtype)
        return jnp.einsum("gaij,ac->gaicj", w4, eye).reshape(groups, per * bs, per * bs).astype(BF16)

    vec = lambda v: v.reshape(1, c)
    tile = pl.BlockSpec((tc, c), lambda b, n: (b * nt + n, 0))
    vmem = 4 * tc * c * 4 + 2 * tc * c * 2 + 3 * tc * c * 4 + 8 * tc * c * 4 + (4 << 20)
    return pl.pallas_call(
        functools.partial(_rglru_kernel, tc=tc, groups=groups),
        grid=(t // seq, nt),
        in_specs=[tile, tile, _const_spec((CONV_W, c)), _const_spec((1, c)),
                  _const_spec((groups, per * bs, per * bs)), _const_spec((groups, per * bs, per * bs)),
                  _const_spec((1, c)), _const_spec((1, c)), _const_spec((1, c))],
        out_specs=tile,
        out_shape=jax.ShapeDtypeStruct((t, c), BF16),
        scratch_shapes=[pltpu.VMEM((tc + SUBLANES, c), F32), pltpu.VMEM((tc, c), F32),
                        pltpu.VMEM((tc, c), F32), pltpu.VMEM((SUBLANES, c), F32)],
        compiler_params=_params(("arbitrary", "arbitrary"), vmem),
        name="rglru",
    )(xa, gg, conv_w, vec(conv_b), block_diag(rg_wa), block_diag(rg_wx), vec(rg_ba), vec(rg_bx), vec(rg_lam))


def _gla_kernel(q_ref, k_ref, la_ref, v_ref, gs_ref, ng_ref, o_ref, st_ref, *, tc):
    @pl.when(pl.program_id(1) == 0)
    def _():
        st_ref[...] = jnp.zeros(st_ref.shape, F32)

    ch = GLA_CHUNK
    ri = lax.broadcasted_iota(jnp.int32, (ch, ch), 0)
    ci = lax.broadcasted_iota(jnp.int32, (ch, ch), 1)
    causal = ri >= ci
    tri = jnp.where(causal, 1.0, 0.0).astype(BF16)
    ng = ng_ref[...]
    for c in range(tc // ch):
        rs = slice(c * ch, (c + 1) * ch)
        for h in range(GLA_HEADS):
            ks = slice(h * GLA_DK, (h + 1) * GLA_DK)
            vs = slice(h * GLA_DV, (h + 1) * GLA_DV)
            la = la_ref[rs, ks]
            hi = la.astype(BF16)
            r1 = la - hi.astype(F32)
            mid = r1.astype(BF16)
            lo = (r1 - mid.astype(F32)).astype(BF16)
            bc = _dot(tri, hi) + _dot(tri, mid) + _dot(tri, lo)
            b_last = bc[ch - 1:ch, :]
            e_pos = jnp.exp(bc)
            e_neg = jnp.exp(-bc)
            kk = k_ref[rs, ks]
            qd = (q_ref[rs, ks] * e_pos).astype(BF16)
            ki = (kk * e_neg).astype(BF16)
            ke = (kk * jnp.exp(b_last - bc)).astype(BF16)
            vc = v_ref[rs, vs]
            st = st_ref[h]
            s = jnp.where(causal, _dot_nt(qd, ki), 0.0).astype(BF16)
            o = _dot(s, vc) + _dot_nt(qd, st.astype(BF16))
            st_ref[h] = st * jnp.exp(b_last) + _dot_tn(vc, ke)
            o_ref[rs, vs] = (_rms_norm(o, ng) * gs_ref[rs, vs]).astype(o_ref.dtype)


def _gla(q, k, la, v, gs, norm_g, seq):
    t = q.shape[0]
    tc = min(TC_GLA, seq)
    nt = seq // tc
    hk = GLA_HEADS * GLA_DK
    hv = GLA_HEADS * GLA_DV
    tile_k = pl.BlockSpec((tc, hk), lambda b, n: (b * nt + n, 0))
    tile_v = pl.BlockSpec((tc, hv), lambda b, n: (b * nt + n, 0))
    vmem = 2 * tc * (3 * hk * 4 + hv * 2 + hv * 4 + hv * 2) + GLA_HEADS * GLA_DV * GLA_DK * 4 + (16 << 20)
    return pl.pallas_call(
        functools.partial(_gla_kernel, tc=tc),
        grid=(t // seq, nt),
        in_specs=[tile_k, tile_k, tile_k, tile_v, tile_v, _const_spec((1, GLA_DV))],
        out_specs=tile_v,
        out_shape=jax.ShapeDtypeStruct((t, hv), BF16),
        scratch_shapes=[pltpu.VMEM((GLA_HEADS, GLA_DV, GLA_DK), F32)],
        compiler_params=_params(("arbitrary", "arbitrary"), vmem),
        name="gla",
    )(q, k, la, v, gs, norm_g.reshape(1, GLA_DV))


def _odd_in_kernel(x_ref, mod_ref, pos_ref, freq_ref, sgn_ref, wcq, wckv, wkp, wkps, qg, kvg,
                   wqn, wqp, wqps, wkn, wvt, qn_o, qp_o, kn_o, kp_o, vt_o, *, row, scale):
    u = _modulated(x_ref[...], mod_ref, row).astype(BF16)
    cqn = _rms_norm(_dot(u, wcq[...]), qg[...]).astype(BF16)
    ckvn = _rms_norm(_dot(u, wckv[...]), kvg[...]).astype(BF16)
    ang = pos_ref[...].astype(F32) * freq_ref[...]
    cos = jnp.cos(ang)
    sin = jnp.sin(ang) * sgn_ref[...]
    kp_o[...] = (_dot(u, wkp[...]) * cos + _dot(u, wkps[...]) * sin).astype(BF16)
    cos_h = jnp.concatenate([cos] * MLA_HEADS, axis=1)
    sin_h = jnp.concatenate([sin] * MLA_HEADS, axis=1)
    qn_o[...] = (_dot(cqn, wqn[...]) * scale).astype(BF16)
    qp_o[...] = ((_dot(cqn, wqp[...]) * cos_h + _dot(cqn, wqps[...]) * sin_h) * scale).astype(BF16)
    kn_o[...] = _dot(ckvn, wkn[...]).astype(BF16)
    vt_o[...] = _dot_nt(wvt[...], ckvn).astype(BF16)


def _swap_halves(w):
    half = w.shape[-1] // 2
    return jnp.concatenate([w[..., half:], w[..., :half]], axis=-1)


def _odd_in(x, mod_l, row, positions, w_in, q_norm_g, w_q_up, kv_norm_g, w_kv_up, seq):
    t, d = x.shape
    tm = min(TM_ODD_IN, seq)
    tpb = seq // tm
    hn = MLA_HEADS * MLA_NOPE
    hv = MLA_HEADS * MLA_V
    pad_r = LANES - MLA_ROPE
    wb = w_in.astype(BF16)
    wcq = wb[:, :MLA_Q_RANK]
    wckv = wb[:, MLA_Q_RANK:MLA_Q_RANK + MLA_KV_RANK]
    wkpe = wb[:, MLA_Q_RANK + MLA_KV_RANK:]
    wkp = jnp.pad(wkpe, ((0, 0), (0, pad_r)))
    wkps = jnp.pad(_swap_halves(wkpe), ((0, 0), (0, pad_r)))
    wq = w_q_up.astype(BF16).reshape(MLA_Q_RANK, MLA_HEADS, MLA_NOPE + MLA_ROPE)
    wqn = wq[:, :, :MLA_NOPE].reshape(MLA_Q_RANK, hn)
    wq_pe = wq[:, :, MLA_NOPE:]
    pad3 = ((0, 0), (0, 0), (0, pad_r))
    wqp = jnp.pad(wq_pe, pad3).reshape(MLA_Q_RANK, MLA_HEADS * LANES)
    wqps = jnp.pad(_swap_halves(wq_pe), pad3).reshape(MLA_Q_RANK, MLA_HEADS * LANES)
    wkv = w_kv_up.astype(BF16).reshape(MLA_KV_RANK, MLA_HEADS, MLA_NOPE + MLA_V)
    wkn = wkv[:, :, :MLA_NOPE].reshape(MLA_KV_RANK, hn)
    wvt = wkv[:, :, MLA_NOPE:].reshape(MLA_KV_RANK, hv).T
    half = MLA_ROPE // 2
    freqs = ROPE_THETA ** (-jnp.arange(half, dtype=F32) / half)
    zeros = jnp.zeros((pad_r,), F32)
    freq_row = jnp.concatenate([freqs, freqs, zeros]).reshape(1, LANES)
    sgn_row = jnp.concatenate([-jnp.ones((half,), F32), jnp.ones((half,), F32), zeros]).reshape(1, LANES)
    ws = [wcq, wckv, wkp, wkps, q_norm_g.reshape(1, -1), kv_norm_g.reshape(1, -1), wqn, wqp, wqps, wkn, wvt]
    widths = [hn, MLA_HEADS * LANES, hn, LANES]
    scale = math.log2(math.e) * (MLA_NOPE + MLA_ROPE) ** -0.5
    vmem = 2 * sum(w.size * 2 for w in ws) + 2 * tm * sum(widths) * 2 + 16 * tm * d * 4 + (8 << 20)
    return pl.pallas_call(
        functools.partial(_odd_in_kernel, row=row, scale=scale),
        grid=(t // tm,),
        in_specs=([pl.BlockSpec((tm, d), lambda r: (r, 0)),
                   pl.BlockSpec((None, N_MOD, d), lambda r: (r // tpb, 0, 0)),
                   pl.BlockSpec((tm, 1), lambda r: (r, 0)),
                   _const_spec((1, LANES)), _const_spec((1, LANES))]
                  + [_const_spec(w.shape) for w in ws]),
        out_specs=([pl.BlockSpec((tm, w), lambda r: (r, 0)) for w in widths]
                   + [pl.BlockSpec((hv, tm), lambda r: (0, r))]),
        out_shape=([jax.ShapeDtypeStruct((t, w), BF16) for w in widths]
                   + [jax.ShapeDtypeStruct((hv, t), BF16)]),
        compiler_params=_params(("arbitrary",), vmem),
        name="odd_in_proj",
    )(x, mod_l, positions.reshape(t, 1), freq_row, sgn_row, *ws)


def _attn_kernel(qn_ref, qp_ref, kn_ref, kp_ref, vt_ref, o_ref, s_ref, p_ref, acc_ref, *, tq):
    i = pl.program_id(2)
    q = jnp.concatenate([qn_ref[...], qp_ref[...]], axis=1)

    def scores(t):
        k0 = pl.multiple_of(t * tq, tq)
        kt = jnp.concatenate([kn_ref[pl.ds(k0, tq), :], kp_ref[pl.ds(k0, tq), :]], axis=1)
        return _dot_nt(kt, q)

    def pv(t, slot):
        k0 = pl.multiple_of(t * tq, tq)
        return _dot(vt_ref[:, pl.ds(k0, tq)], p_ref[slot])

    def step(t, slot, carry, last):
        m_prev, l_prev = carry
        if not last:
            s_ref[1 - slot] = scores(t + 1)
        pv_prev = pv(jnp.maximum(t - 1, 0), 1 - slot)
        st = s_ref[slot]
        if last:
            key = lax.broadcasted_iota(jnp.int32, st.shape, 0)
            qry = lax.broadcasted_iota(jnp.int32, st.shape, 1)
            st = jnp.where(key <= qry, st, -jnp.inf)
        m_new = jnp.maximum(m_prev, jnp.max(st, axis=0, keepdims=True))
        alpha = jnp.exp2(m_prev - m_new)
        p = jnp.exp2(st - m_new)
        l_new = alpha * l_prev + jnp.sum(p, axis=0, keepdims=True)
        p_ref[slot] = p.astype(BF16)
        acc = alpha * (acc_ref[...] + pv_prev)
        if last:
            acc = acc + pv(t, slot)
            o_ref[...] = (acc / l_new).T.astype(o_ref.dtype)
        else:
            acc_ref[...] = acc
        return m_new, l_new

    s_ref[0] = scores(0)
    p_ref[1] = jnp.zeros(p_ref.shape[1:], BF16)
    acc_ref[...] = jnp.zeros(acc_ref.shape, F32)
    init = (jnp.full((1, tq), -jnp.inf, F32), jnp.zeros((1, tq), F32))

    def pair(n, carry):
        carry = step(2 * n, 0, carry, False)
        return step(2 * n + 1, 1, carry, False)

    carry = lax.fori_loop(0, lax.shift_right_logical(i, 1), pair, init)
    odd = (i & 1) == 1

    @pl.when(odd)
    def _():
        step(i, 1, step(i - 1, 0, carry, False), True)

    @pl.when(jnp.logical_not(odd))
    def _():
        step(i, 0, carry, True)


def _attention(qn, qp, kn, kp, vt, seq):
    t = qn.shape[0]
    bsz = t // seq
    tq = min(TQ_ATTN, seq)
    nq = seq // tq
    q_spec = pl.BlockSpec((tq, LANES), lambda b, h, i: (b * nq + i, h))
    kn_spec = pl.BlockSpec((seq, LANES), lambda b, h, i: (b, h))
    kp_spec = pl.BlockSpec((seq, LANES), lambda b, h, i: (b, 0))
    vt_spec = pl.BlockSpec((LANES, seq), lambda b, h, i: (h, b))
    vmem = 2 * 3 * seq * LANES * 2 + 6 * tq * LANES * 2 + 8 * tq * tq * 4 + 3 * tq * LANES * 4 + (4 << 20)
    return pl.pallas_call(
        functools.partial(_attn_kernel, tq=tq),
        grid=(bsz, MLA_HEADS, nq),
        in_specs=[q_spec, q_spec, kn_spec, kp_spec, vt_spec],
        out_specs=q_spec,
        out_shape=jax.ShapeDtypeStruct((t, MLA_HEADS * MLA_V), BF16),
        scratch_shapes=[pltpu.VMEM((2, tq, tq), F32), pltpu.VMEM((2, tq, tq), BF16),
                        pltpu.VMEM((MLA_V, tq), F32)],
        compiler_params=_params(("arbitrary", "arbitrary", "arbitrary"), vmem),
        name="mla_attention",
    )(qn, qp, kn, kp, vt)


def kernel(x, c, positions, mod_w, mod_b, ffn_in, ffn_out, ln_g, ln_b, even_w_in, conv_w, conv_b, rg_wa, rg_ba, rg_wx, rg_bx, rg_lam, gla_wa2, gla_ba, gla_norm_g, even_w_out, odd_w_in, q_norm_g, w_q_up, kv_norm_g, w_kv_up, odd_w_out):
    bsz, seq, d = x.shape
    depth = mod_w.shape[0]
    alpha = (2.0 * depth) ** 0.25
    mod = _mod_vectors(c, mod_w, mod_b)
    h = x.reshape(bsz * seq, d)
    for l in range(depth):
        mod_l = mod[l]
        h = _ffn_sublayer(h, mod_l, 0, ffn_in[l, 0], ffn_out[l, 0], ln_g[l, 0], ln_b[l, 0], alpha, seq)
        if l % 2 == 0:
            e = l // 2
            xa, gg, q, k, v, gs, la = _even_in(h, mod_l, 3, even_w_in[e], gla_wa2[e], gla_ba[e], seq)
            ya = _rglru(xa, gg, conv_w[e], conv_b[e], rg_wa[e], rg_ba[e], rg_wx[e], rg_bx[e], rg_lam[e], seq)
            yb = _gla(q, k, la, v, gs, gla_norm_g[e], seq)
            wo = even_w_out[e].astype(BF16)
            rgw = ya.shape[1]
            h = _proj_deepnorm(h, mod_l, 3, [ya, yb], [wo[:rgw], wo[rgw:]], ln_g[l, 1], ln_b[l, 1], alpha, seq)
        else:
            o = l // 2
            qn, qp, kn, kp, v = _odd_in(h, mod_l, 3, positions, odd_w_in[o], q_norm_g[o], w_q_up[o],
                                        kv_norm_g[o], w_kv_up[o], seq)
            att = _attention(qn, qp, kn, kp, v, seq)
            h = _proj_deepnorm(h, mod_l, 3, [att], [odd_w_out[o].astype(BF16)], ln_g[l, 1], ln_b[l, 1], alpha, seq)
        h = _ffn_sublayer(h, mod_l, 6, ffn_in[l, 1], ffn_out[l, 1], ln_g[l, 2], ln_b[l, 2], alpha, seq)
    return h.reshape(bsz, seq, d)
```

```python
import functools
import math

import jax
import jax.numpy as jnp
from jax import lax
from jax.experimental import pallas as pl
from jax.experimental.pallas import tpu as pltpu

F32 = jnp.float32
BF16 = jnp.bfloat16

LN_EPS = 1e-5
RG_BLOCKS = 16
RG_C = 8.0
CONV_W = 4
GLA_HEADS = 4
GLA_DK = 128
GLA_DV = 256
GLA_RANK = 16
GLA_TAU = 16.0
GLA_CHUNK = 64
MLA_HEADS = 8
MLA_NOPE = 128
MLA_ROPE = 64
MLA_V = 128
MLA_Q_RANK = 256
MLA_KV_RANK = 128
ROPE_THETA = 10000.0
N_MOD = 9

LANES = 128
SUBLANES = 8
VMEM_CAP = 56 << 20

TM_FFN = 512
TM_PROJ = 512
TM_EVEN_IN = 256
TM_ODD_IN = 256
TC_RGLRU = 512
TC_GLA = 256
TQ_ATTN = 1024
FF_CHUNK = 256


def _params(semantics, vmem_bytes):
    return pltpu.CompilerParams(dimension_semantics=semantics,
                                vmem_limit_bytes=int(min(max(vmem_bytes, 16 << 20), VMEM_CAP)))


def _const_spec(shape):
    nd = len(shape)
    return pl.BlockSpec(shape, lambda *_: (0,) * nd, pipeline_mode=pl.Buffered(1))


def _dot(a, b):
    return jnp.dot(a, b, preferred_element_type=F32)


def _dot_nt(a, b):
    return lax.dot_general(a, b, (((1,), (1,)), ((), ())), preferred_element_type=F32)


def _dot_tn(a, b):
    return lax.dot_general(a, b, (((0,), (0,)), ((), ())), preferred_element_type=F32)


def _sigmoid(x):
    return 1.0 / (1.0 + jnp.exp(-x))


def _silu(x):
    return x * _sigmoid(x)


def _softplus(x):
    return jnp.maximum(x, 0.0) + jnp.log1p(jnp.exp(-jnp.abs(x)))


def _log_sigmoid(x):
    return jnp.minimum(x, 0.0) - jnp.log1p(jnp.exp(-jnp.abs(x)))


def _gelu_tanh(x):
    c = math.sqrt(2.0 / math.pi)
    return x * (0.5 * (1.0 + jnp.tanh(c * (x + 0.044715 * (x * x * x)))))


def _layer_norm(y, g, b):
    mu = jnp.mean(y, axis=-1, keepdims=True)
    d = y - mu
    var = jnp.mean(d * d, axis=-1, keepdims=True)
    return d * lax.rsqrt(var + LN_EPS) * g + b


def _rms_norm(y, g):
    return y * lax.rsqrt(jnp.mean(y * y, axis=-1, keepdims=True) + LN_EPS) * g


def _modulated(x, mod_ref, row):
    return x * (1.0 + mod_ref[row + 1:row + 2, :]) + mod_ref[row:row + 1, :]


def _mod_kernel(c_ref, w_ref, b_ref, o_ref):
    ca = _silu(c_ref[...]).astype(BF16)
    o_ref[...] = _dot(ca, w_ref[...].astype(BF16)) + b_ref[...]


def _mod_vectors(c, mod_w, mod_b):
    depth, d, n = mod_w.shape
    bsz = c.shape[0]
    tn = d
    out = pl.pallas_call(
        _mod_kernel,
        grid=(depth, n // tn),
        in_specs=[pl.BlockSpec((bsz, d), lambda l, j: (0, 0)),
                  pl.BlockSpec((None, d, tn), lambda l, j: (l, 0, j)),
                  pl.BlockSpec((None, 1, tn), lambda l, j: (l, 0, j))],
        out_specs=pl.BlockSpec((None, bsz, tn), lambda l, j: (l, 0, j)),
        out_shape=jax.ShapeDtypeStruct((depth, bsz, n), F32),
        compiler_params=_params(("arbitrary", "arbitrary"), 4 * d * tn * 4),
        name="adaln_mod",
    )(c, mod_w, mod_b.reshape(depth, 1, n))
    return out.reshape(depth, bsz, N_MOD, d)


def _ffn_kernel(x_ref, mod_ref, wg_ref, wu_ref, wo_ref, lng_ref, lnb_ref, o_ref, act_ref,
                *, row, alpha, chunk):
    x = x_ref[...]
    u = _modulated(x, mod_ref, row).astype(BF16)
    d_ff = wg_ref.shape[1]
    for j in range(d_ff // chunk):
        sl = slice(j * chunk, (j + 1) * chunk)
        g = _dot(u, wg_ref[:, sl])
        p = _dot(u, wu_ref[:, sl])
        act_ref[:, sl] = (_silu(g) * p).astype(BF16)
    f = _dot(act_ref[...], wo_ref[...])
    gate = mod_ref[row + 2:row + 3, :]
    y = alpha * x + (0.5 * (1.0 + gate)) * f
    o_ref[...] = _layer_norm(y, lng_ref[...], lnb_ref[...])


def _ffn_sublayer(x, mod_l, row, w_in, w_out, ln_g, ln_b, alpha, seq):
    t, d = x.shape
    d_ff = w_out.shape[0]
    tm = min(TM_FFN, seq)
    tpb = seq // tm
    wg = w_in[:, :d_ff].astype(BF16)
    wu = w_in[:, d_ff:].astype(BF16)
    wo = w_out.astype(BF16)
    vmem = 3 * d * d_ff * 2 + 4 * tm * d * 4 + tm * d_ff * 2 + 6 * tm * d * 4 + (8 << 20)
    return pl.pallas_call(
        functools.partial(_ffn_kernel, row=row, alpha=alpha, chunk=min(FF_CHUNK, d_ff)),
        grid=(t // tm,),
        in_specs=[pl.BlockSpec((tm, d), lambda r: (r, 0)),
                  pl.BlockSpec((None, N_MOD, d), lambda r: (r // tpb, 0, 0)),
                  _const_spec((d, d_ff)), _const_spec((d, d_ff)), _const_spec((d_ff, d)),
                  _const_spec((1, d)), _const_spec((1, d))],
        out_specs=pl.BlockSpec((tm, d), lambda r: (r, 0)),
        out_shape=jax.ShapeDtypeStruct((t, d), F32),
        scratch_shapes=[pltpu.VMEM((tm, d_ff), BF16)],
        compiler_params=_params(("arbitrary",), vmem),
        name="ffn_sublayer",
    )(x, mod_l, wg, wu, wo, ln_g.reshape(1, d), ln_b.reshape(1, d))


def _proj_dn_kernel(*refs, n_in, row, alpha):
    x_ref, mod_ref = refs[0], refs[1]
    a_refs = refs[2:2 + n_in]
    w_refs = refs[2 + n_in:2 + 2 * n_in]
    lng_ref, lnb_ref, o_ref = refs[2 + 2 * n_in:]
    m = _dot(a_refs[0][...], w_refs[0][...])
    for a_ref, w_ref in zip(a_refs[1:], w_refs[1:]):
        m = m + _dot(a_ref[...], w_ref[...])
    gate = mod_ref[row + 2:row + 3, :]
    y = alpha * x_ref[...] + (1.0 + gate) * m
    o_ref[...] = _layer_norm(y, lng_ref[...], lnb_ref[...])


def _proj_deepnorm(x, mod_l, row, acts, weights, ln_g, ln_b, alpha, seq):
    t, d = x.shape
    tm = min(TM_PROJ, seq)
    tpb = seq // tm
    n_in = len(acts)
    row_spec = lambda w: pl.BlockSpec((tm, w), lambda r: (r, 0))
    vmem = sum(2 * w.size * 2 for w in weights) + sum(2 * tm * a.shape[1] * 2 for a in acts) + 8 * tm * d * 4 + (4 << 20)
    return pl.pallas_call(
        functools.partial(_proj_dn_kernel, n_in=n_in, row=row, alpha=alpha),
        grid=(t // tm,),
        in_specs=([row_spec(d), pl.BlockSpec((None, N_MOD, d), lambda r: (r // tpb, 0, 0))]
                  + [row_spec(a.shape[1]) for a in acts]
                  + [_const_spec(w.shape) for w in weights]
                  + [_const_spec((1, d)), _const_spec((1, d))]),
        out_specs=row_spec(d),
        out_shape=jax.ShapeDtypeStruct((t, d), F32),
        compiler_params=_params(("arbitrary",), vmem),
        name="proj_deepnorm",
    )(x, mod_l, *acts, *weights, ln_g.reshape(1, d), ln_b.reshape(1, d))


def _even_in_kernel(x_ref, mod_ref, wxa, wga, wq, wk, wv, wg, wza, wa2, ba,
                    xa_o, gg_o, q_o, k_o, v_o, gs_o, la_o, *, row):
    u = _modulated(x_ref[...], mod_ref, row).astype(BF16)
    xa_o[...] = _dot(u, wxa[...])
    gg_o[...] = _gelu_tanh(_dot(u, wga[...]))
    q_o[...] = _dot(u, wq[...]) * (GLA_DK ** -0.5)
    k_o[...] = _dot(u, wk[...])
    v_o[...] = _dot(u, wv[...]).astype(BF16)
    gs_o[...] = _silu(_dot(u, wg[...]))
    za = _dot(u, wza[...]).astype(BF16)
    z = _dot(za, wa2[...]) + ba[...]
    la_o[...] = _log_sigmoid(z) * (1.0 / GLA_TAU)


def _even_in(x, mod_l, row, w_in, gla_wa2, gla_ba, seq):
    t, d = x.shape
    tm = min(TM_EVEN_IN, seq)
    tpb = seq // tm
    hk = GLA_HEADS * GLA_DK
    hv = GLA_HEADS * GLA_DV
    offs = [0, d, 2 * d, 2 * d + hk, 2 * d + 2 * hk, 2 * d + 2 * hk + hv, 2 * d + 2 * hk + 2 * hv]
    wb = w_in.astype(BF16)
    wxa, wga, wq, wk, wv, wg = [wb[:, offs[i]:offs[i + 1]] for i in range(6)]
    wza = jnp.pad(wb[:, offs[6]:], ((0, 0), (0, LANES - GLA_RANK)))
    wa2 = jnp.pad(gla_wa2.astype(BF16), ((0, LANES - GLA_RANK), (0, 0)))
    ws = [wxa, wga, wq, wk, wv, wg, wza, wa2]
    widths = [d, d, hk, hk, hv, hv, hk]
    dtypes = [F32, F32, F32, F32, BF16, F32, F32]
    vmem = sum(w.size * 2 for w in ws) + 2 * tm * sum(widths) * 4 + 4 * tm * d * 4 + (8 << 20)
    return pl.pallas_call(
        functools.partial(_even_in_kernel, row=row),
        grid=(t // tm,),
        in_specs=([pl.BlockSpec((tm, d), lambda r: (r, 0)),
                   pl.BlockSpec((None, N_MOD, d), lambda r: (r // tpb, 0, 0))]
                  + [_const_spec(w.shape) for w in ws] + [_const_spec((1, hk))]),
        out_specs=[pl.BlockSpec((tm, w), lambda r: (r, 0)) for w in widths],
        out_shape=[jax.ShapeDtypeStruct((t, w), dt) for w, dt in zip(widths, dtypes)],
        compiler_params=_params(("arbitrary",), vmem),
        name="even_in_proj",
    )(x, mod_l, *ws, gla_ba.reshape(1, hk))


def _rglru_kernel(xa_ref, gg_ref, cw_ref, cb_ref, wa_ref, wx_ref, ba_ref, bx_ref, lam_ref,
                  o_ref, xpad_ref, a_ref, u_ref, carry_ref, *, tc, groups):
    first = pl.program_id(1) == 0

    @pl.when(first)
    def _():
        xpad_ref[0:SUBLANES, :] = jnp.zeros((SUBLANES, xpad_ref.shape[1]), F32)
        carry_ref[...] = jnp.zeros(carry_ref.shape, F32)

    x = xa_ref[...]
    xpad_ref[SUBLANES:, :] = x
    xc = cw_ref[CONV_W - 1:CONV_W, :] * x + cb_ref[...]
    for dlt in range(1, CONV_W):
        xc = xc + cw_ref[CONV_W - 1 - dlt:CONV_W - dlt, :] * xpad_ref[pl.ds(SUBLANES - dlt, tc), :]
    xpad_ref[0:SUBLANES, :] = x[tc - SUBLANES:, :]

    xb = xc.astype(BF16)
    gw = xb.shape[1] // groups
    neg_c_sp = -RG_C * _softplus(-lam_ref[...])
    for g in range(groups):
        sl = slice(g * gw, (g + 1) * gw)
        r = _sigmoid(_dot(xb[:, sl], wa_ref[g]) + ba_ref[:, sl])
        i = _sigmoid(_dot(xb[:, sl], wx_ref[g]) + bx_ref[:, sl])
        log_a = neg_c_sp[:, sl] * r
        a = jnp.exp(log_a)
        a_ref[:, sl] = a
        u_ref[:, sl] = jnp.sqrt(-jnp.tanh(log_a) * (a * a + 1.0)) * (i * xc[:, sl])

    width = a_ref.shape[1]
    rows = lax.broadcasted_iota(jnp.int32, (SUBLANES, width), 0)

    def body(n, carry):
        r0 = pl.multiple_of(n * SUBLANES, SUBLANES)
        a = a_ref[pl.ds(r0, SUBLANES), :]
        b = u_ref[pl.ds(r0, SUBLANES), :]
        for dlt in (1, 2, 4):
            keep = rows >= dlt
            a_sh = jnp.where(keep, pltpu.roll(a, dlt, 0), 1.0)
            b_sh = jnp.where(keep, pltpu.roll(b, dlt, 0), 0.0)
            b = a * b_sh + b
            a = a * a_sh
        h = b + a * carry
        o_ref[pl.ds(r0, SUBLANES), :] = (h * gg_ref[pl.ds(r0, SUBLANES), :]).astype(o_ref.dtype)
        return jnp.broadcast_to(h[SUBLANES - 1:SUBLANES, :], h.shape)

    carry_ref[...] = lax.fori_loop(0, tc // SUBLANES, body, carry_ref[...], unroll=2)


def _rglru(xa, gg, conv_w, conv_b, rg_wa, rg_ba, rg_wx, rg_bx, rg_lam, seq):
    t, c = xa.shape
    tc = min(TC_RGLRU, seq)
    nt = seq // tc
    bs = c // RG_BLOCKS
    per = (2 * LANES) // bs
    groups = RG_BLOCKS // per

    def block_diag(w):
        w4 = w.reshape(groups, per, bs, bs)
        eye = jnp.eye(per, dtype=w.dtype)
        return jnp.einsum("gaij,ac->gaicj", w4, eye).reshape(groups, per * bs, per * bs).astype(BF16)

    vec = lambda v: v.reshape(1, c)
    tile = pl.BlockSpec((tc, c), lambda b, n: (b * nt + n, 0))
    vmem = 4 * tc * c * 4 + 2 * tc * c * 2 + 3 * tc * c * 4 + 8 * tc * c * 4 + (4 << 20)
    return pl.pallas_call(
        functools.partial(_rglru_kernel, tc=tc, groups=groups),
        grid=(t // seq, nt),
        in_specs=[tile, tile, _const_spec((CONV_W, c)), _const_spec((1, c)),
                  _const_spec((groups, per * bs, per * bs)), _const_spec((groups, per * bs, per * bs)),
                  _const_spec((1, c)), _const_spec((1, c)), _const_spec((1, c))],
        out_specs=tile,
        out_shape=jax.ShapeDtypeStruct((t, c), BF16),
        scratch_shapes=[pltpu.VMEM((tc + SUBLANES, c), F32), pltpu.VMEM((tc, c), F32),
                        pltpu.VMEM((tc, c), F32), pltpu.VMEM((SUBLANES, c), F32)],
        compiler_params=_params(("arbitrary", "arbitrary"), vmem),
        name="rglru",
    )(xa, gg, conv_w, vec(conv_b), block_diag(rg_wa), block_diag(rg_wx), vec(rg_ba), vec(rg_bx), vec(rg_lam))


def _gla_kernel(q_ref, k_ref, la_ref, v_ref, gs_ref, ng_ref, o_ref, st_ref, *, tc):
    @pl.when(pl.program_id(1) == 0)
    def _():
        st_ref[...] = jnp.zeros(st_ref.shape, F32)

    ch = GLA_CHUNK
    nc = tc // ch
    hk = GLA_HEADS * GLA_DK
    ri = lax.broadcasted_iota(jnp.int32, (tc, tc), 0)
    ci = lax.broadcasted_iota(jnp.int32, (tc, tc), 1)
    sh = ch.bit_length() - 1
    same_chunk = jnp.right_shift(ri, sh) == jnp.right_shift(ci, sh)
    tri = jnp.where(same_chunk, jnp.where(ri >= ci, 1.0, 0.0), 0.0).astype(BF16)
    causal = lax.broadcasted_iota(jnp.int32, (ch, ch), 0) >= lax.broadcasted_iota(jnp.int32, (ch, ch), 1)
    ng = ng_ref[...]
    units = [(c, h) for c in range(nc) for h in range(GLA_HEADS)]
    rows = lambda c: slice(c * ch, (c + 1) * ch)
    kcol = lambda h: slice(h * GLA_DK, (h + 1) * GLA_DK)
    vcol = lambda h: slice(h * GLA_DV, (h + 1) * GLA_DV)

    la = la_ref[...]
    hi = la.astype(BF16)
    r1 = la - hi.astype(F32)
    mid = r1.astype(BF16)
    lo = (r1 - mid.astype(F32)).astype(BF16)
    bc3 = _dot(tri, jnp.concatenate([hi, mid, lo], axis=1))
    bc = bc3[:, :hk] + bc3[:, hk:2 * hk] + bc3[:, 2 * hk:]
    kk = k_ref[...]
    qd = (q_ref[...] * jnp.exp(bc)).astype(BF16)
    ki = (kk * jnp.exp(-bc)).astype(BF16)
    b_last = [bc[(c + 1) * ch - 1:(c + 1) * ch, :] for c in range(nc)]
    ke = [(kk[rows(c), :] * jnp.exp(b_last[c] - bc[rows(c), :])).astype(BF16) for c in range(nc)]
    dec = {(c, h): jnp.exp(bc[(c + 1) * ch - 1:(c + 1) * ch, kcol(h)]) for c, h in units}

    s = {(c, h): jnp.where(causal, _dot_nt(qd[rows(c), kcol(h)], ki[rows(c), kcol(h)]), 0.0).astype(BF16)
         for c, h in units}
    kv = {(c, h): _dot_tn(v_ref[rows(c), vcol(h)], ke[c][:, kcol(h)]) for c, h in units}
    st_before = {}
    for h in range(GLA_HEADS):
        st = st_ref[h]
        for c in range(nc):
            st_before[c, h] = st.astype(BF16)
            st = st * dec[c, h] + kv[c, h]
        st_ref[h] = st
    for c, h in units:
        o = _dot(s[c, h], v_ref[rows(c), vcol(h)]) + _dot_nt(qd[rows(c), kcol(h)], st_before[c, h])
        o_ref[rows(c), vcol(h)] = (_rms_norm(o, ng) * gs_ref[rows(c), vcol(h)]).astype(o_ref.dtype)


def _gla(q, k, la, v, gs, norm_g, seq):
    t = q.shape[0]
    tc = min(TC_GLA, seq)
    nt = seq // tc
    hk = GLA_HEADS * GLA_DK
    hv = GLA_HEADS * GLA_DV
    tile_k = pl.BlockSpec((tc, hk), lambda b, n: (b * nt + n, 0))
    tile_v = pl.BlockSpec((tc, hv), lambda b, n: (b * nt + n, 0))
    vmem = 2 * tc * (3 * hk * 4 + hv * 2 + hv * 4 + hv * 2) + GLA_HEADS * GLA_DV * GLA_DK * 4 + (16 << 20)
    return pl.pallas_call(
        functools.partial(_gla_kernel, tc=tc),
        grid=(t // seq, nt),
        in_specs=[tile_k, tile_k, tile_k, tile_v, tile_v, _const_spec((1, GLA_DV))],
        out_specs=tile_v,
        out_shape=jax.ShapeDtypeStruct((t, hv), BF16),
        scratch_shapes=[pltpu.VMEM((GLA_HEADS, GLA_DV, GLA_DK), F32)],
        compiler_params=_params(("arbitrary", "arbitrary"), vmem),
        name="gla",
    )(q, k, la, v, gs, norm_g.reshape(1, GLA_DV))


def _odd_in_kernel(x_ref, mod_ref, pos_ref, freq_ref, sgn_ref, wcq, wckv, wkp, wkps, qg, kvg,
                   wqn, wqp, wqps, wkn, wvt, qn_o, qp_o, kn_o, kp_o, vt_o, *, row, scale):
    u = _modulated(x_ref[...], mod_ref, row).astype(BF16)
    cqn = _rms_norm(_dot(u, wcq[...]), qg[...]).astype(BF16)
    ckvn = _rms_norm(_dot(u, wckv[...]), kvg[...]).astype(BF16)
    ang = pos_ref[...].astype(F32) * freq_ref[...]
    cos = jnp.cos(ang)
    sin = jnp.sin(ang) * sgn_ref[...]
    kp_o[...] = (_dot(u, wkp[...]) * cos + _dot(u, wkps[...]) * sin).astype(BF16)
    cos_h = jnp.concatenate([cos] * MLA_HEADS, axis=1)
    sin_h = jnp.concatenate([sin] * MLA_HEADS, axis=1)
    qn_o[...] = (_dot(cqn, wqn[...]) * scale).astype(BF16)
    qp_o[...] = ((_dot(cqn, wqp[...]) * cos_h + _dot(cqn, wqps[...]) * sin_h) * scale).astype(BF16)
    kn_o[...] = _dot(ckvn, wkn[...]).astype(BF16)
    vt_o[...] = _dot_nt(wvt[...], ckvn).astype(BF16)


def _swap_halves(w):
    half = w.shape[-1] // 2
    return jnp.concatenate([w[..., half:], w[..., :half]], axis=-1)


def _odd_in(x, mod_l, row, positions, w_in, q_norm_g, w_q_up, kv_norm_g, w_kv_up, seq):
    t, d = x.shape
    tm = min(TM_ODD_IN, seq)
    tpb = seq // tm
    hn = MLA_HEADS * MLA_NOPE
    hv = MLA_HEADS * MLA_V
    pad_r = LANES - MLA_ROPE
    wb = w_in.astype(BF16)
    wcq = wb[:, :MLA_Q_RANK]
    wckv = wb[:, MLA_Q_RANK:MLA_Q_RANK + MLA_KV_RANK]
    wkpe = wb[:, MLA_Q_RANK + MLA_KV_RANK:]
    wkp = jnp.pad(wkpe, ((0, 0), (0, pad_r)))
    wkps = jnp.pad(_swap_halves(wkpe), ((0, 0), (0, pad_r)))
    wq = w_q_up.astype(BF16).reshape(MLA_Q_RANK, MLA_HEADS, MLA_NOPE + MLA_ROPE)
    wqn = wq[:, :, :MLA_NOPE].reshape(MLA_Q_RANK, hn)
    wq_pe = wq[:, :, MLA_NOPE:]
    pad3 = ((0, 0), (0, 0), (0, pad_r))
    wqp = jnp.pad(wq_pe, pad3).reshape(MLA_Q_RANK, MLA_HEADS * LANES)
    wqps = jnp.pad(_swap_halves(wq_pe), pad3).reshape(MLA_Q_RANK, MLA_HEADS * LANES)
    wkv = w_kv_up.astype(BF16).reshape(MLA_KV_RANK, MLA_HEADS, MLA_NOPE + MLA_V)
    wkn = wkv[:, :, :MLA_NOPE].reshape(MLA_KV_RANK, hn)
    wvt = wkv[:, :, MLA_NOPE:].reshape(MLA_KV_RANK, hv).T
    half = MLA_ROPE // 2
    freqs = ROPE_THETA ** (-jnp.arange(half, dtype=F32) / half)
    zeros = jnp.zeros((pad_r,), F32)
    freq_row = jnp.concatenate([freqs, freqs, zeros]).reshape(1, LANES)
    sgn_row = jnp.concatenate([-jnp.ones((half,), F32), jnp.ones((half,), F32), zeros]).reshape(1, LANES)
    ws = [wcq, wckv, wkp, wkps, q_norm_g.reshape(1, -1), kv_norm_g.reshape(1, -1), wqn, wqp, wqps, wkn, wvt]
    widths = [hn, MLA_HEADS * LANES, hn, LANES]
    scale = math.log2(math.e) * (MLA_NOPE + MLA_ROPE) ** -0.5
    vmem = 2 * sum(w.size * 2 for w in ws) + 2 * tm * sum(widths) * 2 + 16 * tm * d * 4 + (8 << 20)
    return pl.pallas_call(
        functools.partial(_odd_in_kernel, row=row, scale=scale),
        grid=(t // tm,),
        in_specs=([pl.BlockSpec((tm, d), lambda r: (r, 0)),
                   pl.BlockSpec((None, N_MOD, d), lambda r: (r // tpb, 0, 0)),
                   pl.BlockSpec((tm, 1), lambda r: (r, 0)),
                   _const_spec((1, LANES)), _const_spec((1, LANES))]
                  + [_const_spec(w.shape) for w in ws]),
        out_specs=([pl.BlockSpec((tm, w), lambda r: (r, 0)) for w in widths]
                   + [pl.BlockSpec((hv, tm), lambda r: (0, r))]),
        out_shape=([jax.ShapeDtypeStruct((t, w), BF16) for w in widths]
                   + [jax.ShapeDtypeStruct((hv, t), BF16)]),
        compiler_params=_params(("arbitrary",), vmem),
        name="odd_in_proj",
    )(x, mod_l, positions.reshape(t, 1), freq_row, sgn_row, *ws)


def _attn_kernel(qn_ref, qp_ref, kn_ref, kp_ref, vt_ref, o_ref, s_ref, mx_ref, p_ref, acc_ref, *, tq):
    i = pl.program_id(2)
    q = jnp.concatenate([qn_ref[...], qp_ref[...]], axis=1)

    def put_scores(t, slot):
        k0 = pl.multiple_of(t * tq, tq)
        kt = jnp.concatenate([kn_ref[pl.ds(k0, tq), :], kp_ref[pl.ds(k0, tq), :]], axis=1)
        st = _dot_nt(kt, q)
        s_ref[slot] = st
        mx_ref[slot] = jnp.broadcast_to(jnp.max(st, axis=0, keepdims=True), (SUBLANES, tq))

    def pv(t, slot):
        k0 = pl.multiple_of(t * tq, tq)
        return _dot(vt_ref[:, pl.ds(k0, tq)], p_ref[slot])

    def step(t, slot, carry, last):
        m_prev, l_prev = carry
        st = s_ref[slot]
        if last:
            key = lax.broadcasted_iota(jnp.int32, st.shape, 0)
            qry = lax.broadcasted_iota(jnp.int32, st.shape, 1)
            st = jnp.where(key <= qry, st, -jnp.inf)
            m_cur = jnp.max(st, axis=0, keepdims=True)
        else:
            m_cur = mx_ref[slot][0:1, :]
        m_new = jnp.maximum(m_prev, m_cur)
        alpha = jnp.exp2(m_prev - m_new)
        p = jnp.exp2(st - m_new)
        l_new = alpha * l_prev + jnp.sum(p, axis=0, keepdims=True)
        p_ref[slot] = p.astype(BF16)
        if not last:
            put_scores(t + 1, 1 - slot)
        pv_prev = pv(jnp.maximum(t - 1, 0), 1 - slot)
        acc = alpha * (acc_ref[...] + pv_prev)
        if last:
            acc = acc + pv(t, slot)
            o_ref[...] = (acc / l_new).T.astype(o_ref.dtype)
        else:
            acc_ref[...] = acc
        return m_new, l_new

    put_scores(0, 0)
    p_ref[1] = jnp.zeros(p_ref.shape[1:], BF16)
    acc_ref[...] = jnp.zeros(acc_ref.shape, F32)
    init = (jnp.full((1, tq), -jnp.inf, F32), jnp.zeros((1, tq), F32))

    def pair(n, carry):
        carry = step(2 * n, 0, carry, False)
        return step(2 * n + 1, 1, carry, False)

    carry = lax.fori_loop(0, lax.shift_right_logical(i, 1), pair, init)
    odd = (i & 1) == 1

    @pl.when(odd)
    def _():
        step(i, 1, step(i - 1, 0, carry, False), True)

    @pl.when(jnp.logical_not(odd))
    def _():
        step(i, 0, carry, True)


def _attention(qn, qp, kn, kp, vt, seq):
    t = qn.shape[0]
    bsz = t // seq
    tq = min(TQ_ATTN, seq)
    nq = seq // tq
    q_spec = pl.BlockSpec((tq, LANES), lambda b, h, i: (b * nq + i, h))
    kn_spec = pl.BlockSpec((seq, LANES), lambda b, h, i: (b, h))
    kp_spec = pl.BlockSpec((seq, LANES), lambda b, h, i: (b, 0))
    vt_spec = pl.BlockSpec((LANES, seq), lambda b, h, i: (h, b))
    vmem = 2 * 3 * seq * LANES * 2 + 6 * tq * LANES * 2 + 8 * tq * tq * 4 + 3 * tq * LANES * 4 + (4 << 20)
    return pl.pallas_call(
        functools.partial(_attn_kernel, tq=tq),
        grid=(bsz, MLA_HEADS, nq),
        in_specs=[q_spec, q_spec, kn_spec, kp_spec, vt_spec],
        out_specs=q_spec,
        out_shape=jax.ShapeDtypeStruct((t, MLA_HEADS * MLA_V), BF16),
        scratch_shapes=[pltpu.VMEM((2, tq, tq), F32), pltpu.VMEM((2, SUBLANES, tq), F32),
                        pltpu.VMEM((2, tq, tq), BF16), pltpu.VMEM((MLA_V, tq), F32)],
        compiler_params=_params(("arbitrary", "arbitrary", "arbitrary"), vmem),
        name="mla_attention",
    )(qn, qp, kn, kp, vt)


def kernel(x, c, positions, mod_w, mod_b, ffn_in, ffn_out, ln_g, ln_b, even_w_in, conv_w, conv_b, rg_wa, rg_ba, rg_wx, rg_bx, rg_lam, gla_wa2, gla_ba, gla_norm_g, even_w_out, odd_w_in, q_norm_g, w_q_up, kv_norm_g, w_kv_up, odd_w_out):
    bsz, seq, d = x.shape
    depth = mod_w.shape[0]
    alpha = (2.0 * depth) ** 0.25
    mod = _mod_vectors(c, mod_w, mod_b)
    h = x.reshape(bsz * seq, d)
    for l in range(depth):
        mod_l = mod[l]
        h = _ffn_sublayer(h, mod_l, 0, ffn_in[l, 0], ffn_out[l, 0], ln_g[l, 0], ln_b[l, 0], alpha, seq)
        if l % 2 == 0:
            e = l // 2
            xa, gg, q, k, v, gs, la = _even_in(h, mod_l, 3, even_w_in[e], gla_wa2[e], gla_ba[e], seq)
            ya = _rglru(xa, gg, conv_w[e], conv_b[e], rg_wa[e], rg_ba[e], rg_wx[e], rg_bx[e], rg_lam[e], seq)
            yb = _gla(q, k, la, v, gs, gla_norm_g[e], seq)
            wo = even_w_out[e].astype(BF16)
            rgw = ya.shape[1]
            h = _proj_deepnorm(h, mod_l, 3, [ya, yb], [wo[:rgw], wo[rgw:]], ln_g[l, 1], ln_b[l, 1], alpha, seq)
        else:
            o = l // 2
            qn, qp, kn, kp, v = _odd_in(h, mod_l, 3, positions, odd_w_in[o], q_norm_g[o], w_q_up[o],
                                        kv_norm_g[o], w_kv_up[o], seq)
            att = _attention(qn, qp, kn, kp, v, seq)
            h = _proj_deepnorm(h, mod_l, 3, [att], [odd_w_out[o].astype(BF16)], ln_g[l, 1], ln_b[l, 1], alpha, seq)
        h = _ffn_sublayer(h, mod_l, 6, ffn_in[l, 1], ffn_out[l, 1], ln_g[l, 2], ln_b[l, 2], alpha, seq)
    return h.reshape(bsz, seq, d)
```

```python
import functools
import math

import jax
import jax.numpy as jnp
from jax import lax
from jax.experimental import pallas as pl
from jax.experimental.pallas import tpu as pltpu

F32 = jnp.float32
BF16 = jnp.bfloat16

LN_EPS = 1e-5
RG_BLOCKS = 16
RG_C = 8.0
CONV_W = 4
GLA_HEADS = 4
GLA_DK = 128
GLA_DV = 256
GLA_RANK = 16
GLA_TAU = 16.0
GLA_CHUNK = 64
MLA_HEADS = 8
MLA_NOPE = 128
MLA_ROPE = 64
MLA_V = 128
MLA_Q_RANK = 256
MLA_KV_RANK = 128
ROPE_THETA = 10000.0
N_MOD = 9

LANES = 128
SUBLANES = 8
VMEM_CAP = 56 << 20

TM_FFN = 512
TM_PROJ = 512
TM_EVEN_IN = 512
TM_ODD_IN = 512
TC_RGLRU = 512
TC_GLA = 256
TQ_ATTN = 1024
FF_CHUNK = 256


def _params(semantics, vmem_bytes):
    return pltpu.CompilerParams(dimension_semantics=semantics,
                                vmem_limit_bytes=int(min(max(vmem_bytes, 16 << 20), VMEM_CAP)))


def _const_spec(shape):
    nd = len(shape)
    return pl.BlockSpec(shape, lambda *_: (0,) * nd, pipeline_mode=pl.Buffered(1))


def _dot(a, b):
    return jnp.dot(a, b, preferred_element_type=F32)


def _dot_nt(a, b):
    return lax.dot_general(a, b, (((1,), (1,)), ((), ())), preferred_element_type=F32)


def _dot_tn(a, b):
    return lax.dot_general(a, b, (((0,), (0,)), ((), ())), preferred_element_type=F32)


def _sigmoid(x):
    return 1.0 / (1.0 + jnp.exp(-x))


def _silu(x):
    return x * _sigmoid(x)


def _softplus(x):
    return jnp.maximum(x, 0.0) + jnp.log1p(jnp.exp(-jnp.abs(x)))


def _log_sigmoid(x):
    return jnp.minimum(x, 0.0) - jnp.log1p(jnp.exp(-jnp.abs(x)))


def _gelu_tanh(x):
    c = math.sqrt(2.0 / math.pi)
    return x * (0.5 * (1.0 + jnp.tanh(c * (x + 0.044715 * (x * x * x)))))


def _layer_norm(y, g, b):
    mu = jnp.mean(y, axis=-1, keepdims=True)
    d = y - mu
    var = jnp.mean(d * d, axis=-1, keepdims=True)
    return d * lax.rsqrt(var + LN_EPS) * g + b


def _rms_norm(y, g):
    return y * lax.rsqrt(jnp.mean(y * y, axis=-1, keepdims=True) + LN_EPS) * g


def _modulated(x, mod_ref, row):
    return x * (1.0 + mod_ref[row + 1:row + 2, :]) + mod_ref[row:row + 1, :]


def _mod_kernel(c_ref, w_ref, b_ref, o_ref):
    ca = _silu(c_ref[...]).astype(BF16)
    o_ref[...] = _dot(ca, w_ref[...].astype(BF16)) + b_ref[...]


def _mod_vectors(c, mod_w, mod_b):
    depth, d, n = mod_w.shape
    bsz = c.shape[0]
    tn = d
    out = pl.pallas_call(
        _mod_kernel,
        grid=(depth, n // tn),
        in_specs=[pl.BlockSpec((bsz, d), lambda l, j: (0, 0)),
                  pl.BlockSpec((None, d, tn), lambda l, j: (l, 0, j)),
                  pl.BlockSpec((None, 1, tn), lambda l, j: (l, 0, j))],
        out_specs=pl.BlockSpec((None, bsz, tn), lambda l, j: (l, 0, j)),
        out_shape=jax.ShapeDtypeStruct((depth, bsz, n), F32),
        compiler_params=_params(("arbitrary", "arbitrary"), 4 * d * tn * 4),
        name="adaln_mod",
    )(c, mod_w, mod_b.reshape(depth, 1, n))
    return out.reshape(depth, bsz, N_MOD, d)


def _ffn_kernel(x_ref, mod_ref, wg_ref, wu_ref, wo_ref, lng_ref, lnb_ref, o_ref, act_ref,
                *, row, alpha, chunk):
    x = x_ref[...]
    u = _modulated(x, mod_ref, row).astype(BF16)
    d_ff = wg_ref.shape[1]
    for j in range(d_ff // chunk):
        sl = slice(j * chunk, (j + 1) * chunk)
        g = _dot(u, wg_ref[:, sl])
        p = _dot(u, wu_ref[:, sl])
        act_ref[:, sl] = (_silu(g) * p).astype(BF16)
    f = _dot(act_ref[...], wo_ref[...])
    gate = mod_ref[row + 2:row + 3, :]
    y = alpha * x + (0.5 * (1.0 + gate)) * f
    o_ref[...] = _layer_norm(y, lng_ref[...], lnb_ref[...])


def _ffn_sublayer(x, mod_l, row, w_in, w_out, ln_g, ln_b, alpha, seq):
    t, d = x.shape
    d_ff = w_out.shape[0]
    tm = min(TM_FFN, seq)
    tpb = seq // tm
    wg = w_in[:, :d_ff].astype(BF16)
    wu = w_in[:, d_ff:].astype(BF16)
    wo = w_out.astype(BF16)
    vmem = 3 * d * d_ff * 2 + 4 * tm * d * 4 + tm * d_ff * 2 + 6 * tm * d * 4 + (8 << 20)
    return pl.pallas_call(
        functools.partial(_ffn_kernel, row=row, alpha=alpha, chunk=min(FF_CHUNK, d_ff)),
        grid=(t // tm,),
        in_specs=[pl.BlockSpec((tm, d), lambda r: (r, 0)),
                  pl.BlockSpec((None, N_MOD, d), lambda r: (r // tpb, 0, 0)),
                  _const_spec((d, d_ff)), _const_spec((d, d_ff)), _const_spec((d_ff, d)),
                  _const_spec((1, d)), _const_spec((1, d))],
        out_specs=pl.BlockSpec((tm, d), lambda r: (r, 0)),
        out_shape=jax.ShapeDtypeStruct((t, d), F32),
        scratch_shapes=[pltpu.VMEM((tm, d_ff), BF16)],
        compiler_params=_params(("arbitrary",), vmem),
        name="ffn_sublayer",
    )(x, mod_l, wg, wu, wo, ln_g.reshape(1, d), ln_b.reshape(1, d))


def _proj_dn_kernel(*refs, n_in, row, alpha):
    x_ref, mod_ref = refs[0], refs[1]
    a_refs = refs[2:2 + n_in]
    w_refs = refs[2 + n_in:2 + 2 * n_in]
    lng_ref, lnb_ref, o_ref = refs[2 + 2 * n_in:]
    gate = mod_ref[row + 2:row + 3, :]
    tm = x_ref.shape[0]
    halves = [slice(0, tm // 2), slice(tm // 2, tm)]
    ms = []
    for rs in halves:
        m = _dot(a_refs[0][rs, :], w_refs[0][...])
        for a_ref, w_ref in zip(a_refs[1:], w_refs[1:]):
            m = m + _dot(a_ref[rs, :], w_ref[...])
        ms.append(m)
    for rs, m in zip(halves, ms):
        y = alpha * x_ref[rs, :] + (1.0 + gate) * m
        o_ref[rs, :] = _layer_norm(y, lng_ref[...], lnb_ref[...])


def _proj_deepnorm(x, mod_l, row, acts, weights, ln_g, ln_b, alpha, seq):
    t, d = x.shape
    tm = min(TM_PROJ, seq)
    tpb = seq // tm
    n_in = len(acts)
    row_spec = lambda w: pl.BlockSpec((tm, w), lambda r: (r, 0))
    vmem = sum(2 * w.size * 2 for w in weights) + sum(2 * tm * a.shape[1] * 2 for a in acts) + 8 * tm * d * 4 + (4 << 20)
    return pl.pallas_call(
        functools.partial(_proj_dn_kernel, n_in=n_in, row=row, alpha=alpha),
        grid=(t // tm,),
        in_specs=([row_spec(d), pl.BlockSpec((None, N_MOD, d), lambda r: (r // tpb, 0, 0))]
                  + [row_spec(a.shape[1]) for a in acts]
                  + [_const_spec(w.shape) for w in weights]
                  + [_const_spec((1, d)), _const_spec((1, d))]),
        out_specs=row_spec(d),
        out_shape=jax.ShapeDtypeStruct((t, d), F32),
        compiler_params=_params(("arbitrary",), vmem),
        name="proj_deepnorm",
    )(x, mod_l, *acts, *weights, ln_g.reshape(1, d), ln_b.reshape(1, d))


def _even_in_kernel(x_ref, mod_ref, wxa, wga, wq, wk, wv, wg, wza, wa2, ba,
                    xa_o, gg_o, q_o, k_o, v_o, gs_o, la_o, *, row):
    u = _modulated(x_ref[...], mod_ref, row).astype(BF16)
    xa_o[...] = _dot(u, wxa[...])
    gg_o[...] = _gelu_tanh(_dot(u, wga[...]))
    q_o[...] = _dot(u, wq[...]) * (GLA_DK ** -0.5)
    k_o[...] = _dot(u, wk[...])
    v_o[...] = _dot(u, wv[...]).astype(BF16)
    gs_o[...] = _silu(_dot(u, wg[...]))
    za = _dot(u, wza[...]).astype(BF16)
    z = _dot(za, wa2[...]) + ba[...]
    la_o[...] = _log_sigmoid(z) * (1.0 / GLA_TAU)


def _even_in(x, mod_l, row, w_in, gla_wa2, gla_ba, seq):
    t, d = x.shape
    tm = min(TM_EVEN_IN, seq)
    tpb = seq // tm
    hk = GLA_HEADS * GLA_DK
    hv = GLA_HEADS * GLA_DV
    offs = [0, d, 2 * d, 2 * d + hk, 2 * d + 2 * hk, 2 * d + 2 * hk + hv, 2 * d + 2 * hk + 2 * hv]
    wb = w_in.astype(BF16)
    wxa, wga, wq, wk, wv, wg = [wb[:, offs[i]:offs[i + 1]] for i in range(6)]
    wza = jnp.pad(wb[:, offs[6]:], ((0, 0), (0, LANES - GLA_RANK)))
    wa2 = jnp.pad(gla_wa2.astype(BF16), ((0, LANES - GLA_RANK), (0, 0)))
    ws = [wxa, wga, wq, wk, wv, wg, wza, wa2]
    widths = [d, d, hk, hk, hv, hv, hk]
    dtypes = [F32, F32, F32, F32, BF16, F32, F32]
    vmem = sum(w.size * 2 for w in ws) + 2 * tm * sum(widths) * 4 + 4 * tm * d * 4 + (8 << 20)
    return pl.pallas_call(
        functools.partial(_even_in_kernel, row=row),
        grid=(t // tm,),
        in_specs=([pl.BlockSpec((tm, d), lambda r: (r, 0)),
                   pl.BlockSpec((None, N_MOD, d), lambda r: (r // tpb, 0, 0))]
                  + [_const_spec(w.shape) for w in ws] + [_const_spec((1, hk))]),
        out_specs=[pl.BlockSpec((tm, w), lambda r: (r, 0)) for w in widths],
        out_shape=[jax.ShapeDtypeStruct((t, w), dt) for w, dt in zip(widths, dtypes)],
        compiler_params=_params(("arbitrary",), vmem),
        name="even_in_proj",
    )(x, mod_l, *ws, gla_ba.reshape(1, hk))


def _rglru_kernel(xa_ref, gg_ref, cw_ref, cb_ref, wa_ref, wx_ref, ba_ref, bx_ref, lam_ref,
                  o_ref, xpad_ref, a_ref, u_ref, carry_ref, *, tc, groups):
    first = pl.program_id(1) == 0

    @pl.when(first)
    def _():
        xpad_ref[0:SUBLANES, :] = jnp.zeros((SUBLANES, xpad_ref.shape[1]), F32)
        carry_ref[...] = jnp.zeros(carry_ref.shape, F32)

    x = xa_ref[...]
    xpad_ref[SUBLANES:, :] = x
    xc = cw_ref[CONV_W - 1:CONV_W, :] * x + cb_ref[...]
    for dlt in range(1, CONV_W):
        xc = xc + cw_ref[CONV_W - 1 - dlt:CONV_W - dlt, :] * xpad_ref[pl.ds(SUBLANES - dlt, tc), :]
    xpad_ref[0:SUBLANES, :] = x[tc - SUBLANES:, :]

    xb = xc.astype(BF16)
    gw = xb.shape[1] // groups
    neg_c_sp = -RG_C * _softplus(-lam_ref[...])
    for g in range(groups):
        sl = slice(g * gw, (g + 1) * gw)
        r = _sigmoid(_dot(xb[:, sl], wa_ref[g]) + ba_ref[:, sl])
        i = _sigmoid(_dot(xb[:, sl], wx_ref[g]) + bx_ref[:, sl])
        log_a = neg_c_sp[:, sl] * r
        a = jnp.exp(log_a)
        a_ref[:, sl] = a
        z = -jnp.tanh(log_a) * (a * a + 1.0)
        root = jnp.where(z > 0.0, z * lax.rsqrt(z), 0.0)
        u_ref[:, sl] = root * (i * xc[:, sl])

    width = a_ref.shape[1]
    rows = lax.broadcasted_iota(jnp.int32, (SUBLANES, width), 0)

    def body(n, carry):
        r0 = pl.multiple_of(n * SUBLANES, SUBLANES)
        a = a_ref[pl.ds(r0, SUBLANES), :]
        b = u_ref[pl.ds(r0, SUBLANES), :]
        for dlt in (1, 2, 4):
            keep = rows >= dlt
            a_sh = jnp.where(keep, pltpu.roll(a, dlt, 0), 1.0)
            b_sh = jnp.where(keep, pltpu.roll(b, dlt, 0), 0.0)
            b = a * b_sh + b
            a = a * a_sh
        h = b + a * carry
        o_ref[pl.ds(r0, SUBLANES), :] = (h * gg_ref[pl.ds(r0, SUBLANES), :]).astype(o_ref.dtype)
        return jnp.broadcast_to(h[SUBLANES - 1:SUBLANES, :], h.shape)

    carry_ref[...] = lax.fori_loop(0, tc // SUBLANES, body, carry_ref[...], unroll=2)


def _rglru(xa, gg, conv_w, conv_b, rg_wa, rg_ba, rg_wx, rg_bx, rg_lam, seq):
    t, c = xa.shape
    tc = min(TC_RGLRU, seq)
    nt = seq // tc
    bs = c // RG_BLOCKS
    per = (2 * LANES) // bs
    groups = RG_BLOCKS // per

    def block_diag(w):
        w4 = w.reshape(groups, per, bs, bs)
        eye = jnp.eye(per, dtype=w.dtype)
        return jnp.einsum("gaij,ac->gaicj", w4, eye).reshape(groups, per * bs, per * bs).astype(BF16)

    vec = lambda v: v.reshape(1, c)
    tile = pl.BlockSpec((tc, c), lambda b, n: (b * nt + n, 0))
    vmem = 4 * tc * c * 4 + 2 * tc * c * 2 + 3 * tc * c * 4 + 8 * tc * c * 4 + (4 << 20)
    return pl.pallas_call(
        functools.partial(_rglru_kernel, tc=tc, groups=groups),
        grid=(t // seq, nt),
        in_specs=[tile, tile, _const_spec((CONV_W, c)), _const_spec((1, c)),
                  _const_spec((groups, per * bs, per * bs)), _const_spec((groups, per * bs, per * bs)),
                  _const_spec((1, c)), _const_spec((1, c)), _const_spec((1, c))],
        out_specs=tile,
        out_shape=jax.ShapeDtypeStruct((t, c), BF16),
        scratch_shapes=[pltpu.VMEM((tc + SUBLANES, c), F32), pltpu.VMEM((tc, c), F32),
                        pltpu.VMEM((tc, c), F32), pltpu.VMEM((SUBLANES, c), F32)],
        compiler_params=_params(("arbitrary", "arbitrary"), vmem),
        name="rglru",
    )(xa, gg, conv_w, vec(conv_b), block_diag(rg_wa), block_diag(rg_wx), vec(rg_ba), vec(rg_bx), vec(rg_lam))


def _gla_kernel(q_ref, k_ref, la_ref, v_ref, gs_ref, ng_ref, o_ref, st_ref, *, tc):
    @pl.when(pl.program_id(1) == 0)
    def _():
        st_ref[...] = jnp.zeros(st_ref.shape, F32)

    ch = GLA_CHUNK
    nc = tc // ch
    hk = GLA_HEADS * GLA_DK
    ri = lax.broadcasted_iota(jnp.int32, (tc, tc), 0)
    ci = lax.broadcasted_iota(jnp.int32, (tc, tc), 1)
    sh = ch.bit_length() - 1
    same_chunk = jnp.right_shift(ri, sh) == jnp.right_shift(ci, sh)
    tri = jnp.where(same_chunk, jnp.where(ri >= ci, 1.0, 0.0), 0.0).astype(BF16)
    causal = lax.broadcasted_iota(jnp.int32, (ch, ch), 0) >= lax.broadcasted_iota(jnp.int32, (ch, ch), 1)
    ng = ng_ref[...]
    units = [(c, h) for c in range(nc) for h in range(GLA_HEADS)]
    rows = lambda c: slice(c * ch, (c + 1) * ch)
    kcol = lambda h: slice(h * GLA_DK, (h + 1) * GLA_DK)
    vcol = lambda h: slice(h * GLA_DV, (h + 1) * GLA_DV)

    la = la_ref[...]
    hi = la.astype(BF16)
    r1 = la - hi.astype(F32)
    mid = r1.astype(BF16)
    lo = (r1 - mid.astype(F32)).astype(BF16)
    bc3 = _dot(tri, jnp.concatenate([hi, mid, lo], axis=1))
    bc = bc3[:, :hk] + bc3[:, hk:2 * hk] + bc3[:, 2 * hk:]
    kk = k_ref[...]
    qd = (q_ref[...] * jnp.exp(bc)).astype(BF16)
    ki = (kk * jnp.exp(-bc)).astype(BF16)
    b_last = [bc[(c + 1) * ch - 1:(c + 1) * ch, :] for c in range(nc)]
    ke = [(kk[rows(c), :] * jnp.exp(b_last[c] - bc[rows(c), :])).astype(BF16) for c in range(nc)]
    dec = {(c, h): jnp.exp(bc[(c + 1) * ch - 1:(c + 1) * ch, kcol(h)]) for c, h in units}

    s = {(c, h): jnp.where(causal, _dot_nt(qd[rows(c), kcol(h)], ki[rows(c), kcol(h)]), 0.0).astype(BF16)
         for c, h in units}
    kv = {(c, h): _dot_tn(v_ref[rows(c), vcol(h)], ke[c][:, kcol(h)]) for c, h in units}
    st_before = {}
    for h in range(GLA_HEADS):
        st = st_ref[h]
        for c in range(nc):
            st_before[c, h] = st.astype(BF16)
            st = st * dec[c, h] + kv[c, h]
        st_ref[h] = st
    for c, h in units:
        o = _dot(s[c, h], v_ref[rows(c), vcol(h)]) + _dot_nt(qd[rows(c), kcol(h)], st_before[c, h])
        o_ref[rows(c), vcol(h)] = (_rms_norm(o, ng) * gs_ref[rows(c), vcol(h)]).astype(o_ref.dtype)


def _gla(q, k, la, v, gs, norm_g, seq):
    t = q.shape[0]
    tc = min(TC_GLA, seq)
    nt = seq // tc
    hk = GLA_HEADS * GLA_DK
    hv = GLA_HEADS * GLA_DV
    tile_k = pl.BlockSpec((tc, hk), lambda b, n: (b * nt + n, 0))
    tile_v = pl.BlockSpec((tc, hv), lambda b, n: (b * nt + n, 0))
    vmem = 2 * tc * (3 * hk * 4 + hv * 2 + hv * 4 + hv * 2) + GLA_HEADS * GLA_DV * GLA_DK * 4 + (16 << 20)
    return pl.pallas_call(
        functools.partial(_gla_kernel, tc=tc),
        grid=(t // seq, nt),
        in_specs=[tile_k, tile_k, tile_k, tile_v, tile_v, _const_spec((1, GLA_DV))],
        out_specs=tile_v,
        out_shape=jax.ShapeDtypeStruct((t, hv), BF16),
        scratch_shapes=[pltpu.VMEM((GLA_HEADS, GLA_DV, GLA_DK), F32)],
        compiler_params=_params(("arbitrary", "arbitrary"), vmem),
        name="gla",
    )(q, k, la, v, gs, norm_g.reshape(1, GLA_DV))


def _odd_in_kernel(x_ref, mod_ref, pos_ref, freq_ref, sgn_ref, wcq, wckv, wkp, wkps, qg, kvg,
                   wqn, wqp, wqps, wkn, wvt, qn_o, qp_o, kn_o, kp_o, vt_o, *, row, scale):
    u = _modulated(x_ref[...], mod_ref, row).astype(BF16)
    cqn = _rms_norm(_dot(u, wcq[...]), qg[...]).astype(BF16)
    ckvn = _rms_norm(_dot(u, wckv[...]), kvg[...]).astype(BF16)
    ang = pos_ref[...].astype(F32) * freq_ref[...]
    cos = jnp.cos(ang)
    sin = jnp.sin(ang) * sgn_ref[...]
    kp_o[...] = (_dot(u, wkp[...]) * cos + _dot(u, wkps[...]) * sin).astype(BF16)
    cos_h = jnp.concatenate([cos] * MLA_HEADS, axis=1)
    sin_h = jnp.concatenate([sin] * MLA_HEADS, axis=1)
    qn_o[...] = (_dot(cqn, wqn[...]) * scale).astype(BF16)
    qp_o[...] = ((_dot(cqn, wqp[...]) * cos_h + _dot(cqn, wqps[...]) * sin_h) * scale).astype(BF16)
    kn_o[...] = _dot(ckvn, wkn[...]).astype(BF16)
    vt_o[...] = _dot_nt(wvt[...], ckvn).astype(BF16)


def _swap_halves(w):
    half = w.shape[-1] // 2
    return jnp.concatenate([w[..., half:], w[..., :half]], axis=-1)


def _odd_in(x, mod_l, row, positions, w_in, q_norm_g, w_q_up, kv_norm_g, w_kv_up, seq):
    t, d = x.shape
    tm = min(TM_ODD_IN, seq)
    tpb = seq // tm
    hn = MLA_HEADS * MLA_NOPE
    hv = MLA_HEADS * MLA_V
    pad_r = LANES - MLA_ROPE
    wb = w_in.astype(BF16)
    wcq = wb[:, :MLA_Q_RANK]
    wckv = wb[:, MLA_Q_RANK:MLA_Q_RANK + MLA_KV_RANK]
    wkpe = wb[:, MLA_Q_RANK + MLA_KV_RANK:]
    wkp = jnp.pad(wkpe, ((0, 0), (0, pad_r)))
    wkps = jnp.pad(_swap_halves(wkpe), ((0, 0), (0, pad_r)))
    wq = w_q_up.astype(BF16).reshape(MLA_Q_RANK, MLA_HEADS, MLA_NOPE + MLA_ROPE)
    wqn = wq[:, :, :MLA_NOPE].reshape(MLA_Q_RANK, hn)
    wq_pe = wq[:, :, MLA_NOPE:]
    pad3 = ((0, 0), (0, 0), (0, pad_r))
    wqp = jnp.pad(wq_pe, pad3).reshape(MLA_Q_RANK, MLA_HEADS * LANES)
    wqps = jnp.pad(_swap_halves(wq_pe), pad3).reshape(MLA_Q_RANK, MLA_HEADS * LANES)
    wkv = w_kv_up.astype(BF16).reshape(MLA_KV_RANK, MLA_HEADS, MLA_NOPE + MLA_V)
    wkn = wkv[:, :, :MLA_NOPE].reshape(MLA_KV_RANK, hn)
    wvt = wkv[:, :, MLA_NOPE:].reshape(MLA_KV_RANK, hv).T
    half = MLA_ROPE // 2
    freqs = ROPE_THETA ** (-jnp.arange(half, dtype=F32) / half)
    zeros = jnp.zeros((pad_r,), F32)
    freq_row = jnp.concatenate([freqs, freqs, zeros]).reshape(1, LANES)
    sgn_row = jnp.concatenate([-jnp.ones((half,), F32), jnp.ones((half,), F32), zeros]).reshape(1, LANES)
    ws = [wcq, wckv, wkp, wkps, q_norm_g.reshape(1, -1), kv_norm_g.reshape(1, -1), wqn, wqp, wqps, wkn, wvt]
    widths = [hn, MLA_HEADS * LANES, hn, LANES]
    scale = math.log2(math.e) * (MLA_NOPE + MLA_ROPE) ** -0.5
    vmem = 2 * sum(w.size * 2 for w in ws) + 2 * tm * sum(widths) * 2 + 16 * tm * d * 4 + (8 << 20)
    return pl.pallas_call(
        functools.partial(_odd_in_kernel, row=row, scale=scale),
        grid=(t // tm,),
        in_specs=([pl.BlockSpec((tm, d), lambda r: (r, 0)),
                   pl.BlockSpec((None, N_MOD, d), lambda r: (r // tpb, 0, 0)),
                   pl.BlockSpec((tm, 1), lambda r: (r, 0)),
                   _const_spec((1, LANES)), _const_spec((1, LANES))]
                  + [_const_spec(w.shape) for w in ws]),
        out_specs=([pl.BlockSpec((tm, w), lambda r: (r, 0)) for w in widths]
                   + [pl.BlockSpec((hv, tm), lambda r: (0, r))]),
        out_shape=([jax.ShapeDtypeStruct((t, w), BF16) for w in widths]
                   + [jax.ShapeDtypeStruct((hv, t), BF16)]),
        compiler_params=_params(("arbitrary",), vmem),
        name="odd_in_proj",
    )(x, mod_l, positions.reshape(t, 1), freq_row, sgn_row, *ws)


def _attn_kernel(qn_ref, qp_ref, qn_next_ref, qp_next_ref, kn_ref, kp_ref, vt_ref, o_ref,
                 s_ref, mx_ref, p_ref, acc_ref, *, tq):
    i = pl.program_id(2)
    q = jnp.concatenate([qn_ref[...], qp_ref[...]], axis=1)
    FIRST = 2

    def put_scores(t, slot, qmat):
        k0 = pl.multiple_of(t * tq, tq)
        kt = jnp.concatenate([kn_ref[pl.ds(k0, tq), :], kp_ref[pl.ds(k0, tq), :]], axis=1)
        st = _dot_nt(kt, qmat)
        s_ref[slot] = st
        mx_ref[slot] = jnp.broadcast_to(jnp.max(st, axis=0, keepdims=True), (SUBLANES, tq))

    def pv(t, slot):
        k0 = pl.multiple_of(t * tq, tq)
        return _dot(vt_ref[:, pl.ds(k0, tq)], p_ref[slot])

    def step(t, s_in, s_out, p_in, p_out, carry, first, last):
        m_prev, l_prev = carry
        st = s_ref[s_in]
        if last:
            key = lax.broadcasted_iota(jnp.int32, st.shape, 0)
            qry = lax.broadcasted_iota(jnp.int32, st.shape, 1)
            st = jnp.where(key <= qry, st, -jnp.inf)
            m_cur = jnp.max(st, axis=0, keepdims=True)
        else:
            m_cur = mx_ref[s_in][0:1, :]
        m_new = jnp.maximum(m_prev, m_cur)
        alpha = jnp.exp2(m_prev - m_new)
        p = jnp.exp2(st - m_new)
        l_new = alpha * l_prev + jnp.sum(p, axis=0, keepdims=True)
        p_ref[p_out] = p.astype(BF16)
        if last:
            q_next = jnp.concatenate([qn_next_ref[...], qp_next_ref[...]], axis=1)
            put_scores(0, FIRST, q_next)
        else:
            put_scores(t + 1, s_out, q)
        if first:
            acc = jnp.zeros(acc_ref.shape, F32)
        else:
            acc = alpha * (acc_ref[...] + pv(t - 1, p_in))
        if last:
            acc = acc + pv(t, p_out)
            o_ref[...] = (acc / l_new).T.astype(o_ref.dtype)
        else:
            acc_ref[...] = acc
        return m_new, l_new

    init = (jnp.full((1, tq), -jnp.inf, F32), jnp.zeros((1, tq), F32))

    @pl.when(i == 0)
    def _():
        put_scores(0, FIRST, q)
        step(0, FIRST, None, None, 0, init, True, True)

    @pl.when(i > 0)
    def _():
        carry = step(0, FIRST, 0, None, 0, init, True, False)

        def pair(n, carry):
            carry = step(2 * n + 1, 0, 1, 0, 1, carry, False, False)
            return step(2 * n + 2, 1, 0, 1, 0, carry, False, False)

        rest = i - 1
        carry = lax.fori_loop(0, lax.shift_right_logical(rest, 1), pair, carry)
        odd = (rest & 1) == 1

        @pl.when(odd)
        def _():
            c = step(i - 1, 0, 1, 0, 1, carry, False, False)
            step(i, 1, None, 1, 0, c, False, True)

        @pl.when(jnp.logical_not(odd))
        def _():
            step(i, 0, None, 0, 1, carry, False, True)


def _attention(qn, qp, kn, kp, vt, seq):
    t = qn.shape[0]
    bsz = t // seq
    tq = min(TQ_ATTN, seq)
    nq = seq // tq
    q_spec = pl.BlockSpec((tq, LANES), lambda b, h, i: (b * nq + i, h))
    kn_spec = pl.BlockSpec((seq, LANES), lambda b, h, i: (b, h))
    kp_spec = pl.BlockSpec((seq, LANES), lambda b, h, i: (b, 0))
    vt_spec = pl.BlockSpec((LANES, seq), lambda b, h, i: (h, b))
    q_next_spec = pl.BlockSpec((tq, LANES), lambda b, h, i: (b * nq + jnp.minimum(i + 1, nq - 1), h))
    vmem = 2 * 3 * seq * LANES * 2 + 10 * tq * LANES * 2 + 3 * tq * tq * 4 + 2 * tq * tq * 2 + 8 * tq * LANES * 4 + (8 << 20)
    return pl.pallas_call(
        functools.partial(_attn_kernel, tq=tq),
        grid=(bsz, MLA_HEADS, nq),
        in_specs=[q_spec, q_spec, q_next_spec, q_next_spec, kn_spec, kp_spec, vt_spec],
        out_specs=q_spec,
        out_shape=jax.ShapeDtypeStruct((t, MLA_HEADS * MLA_V), BF16),
        scratch_shapes=[pltpu.VMEM((3, tq, tq), F32), pltpu.VMEM((3, SUBLANES, tq), F32),
                        pltpu.VMEM((2, tq, tq), BF16), pltpu.VMEM((MLA_V, tq), F32)],
        compiler_params=_params(("arbitrary", "arbitrary", "arbitrary"), vmem),
        name="mla_attention",
    )(qn, qp, qn, qp, kn, kp, vt)


def kernel(x, c, positions, mod_w, mod_b, ffn_in, ffn_out, ln_g, ln_b, even_w_in, conv_w, conv_b, rg_wa, rg_ba, rg_wx, rg_bx, rg_lam, gla_wa2, gla_ba, gla_norm_g, even_w_out, odd_w_in, q_norm_g, w_q_up, kv_norm_g, w_kv_up, odd_w_out):
    bsz, seq, d = x.shape
    depth = mod_w.shape[0]
    alpha = (2.0 * depth) ** 0.25
    mod = _mod_vectors(c, mod_w, mod_b)
    h = x.reshape(bsz * seq, d)
    for l in range(depth):
        mod_l = mod[l]
        h = _ffn_sublayer(h, mod_l, 0, ffn_in[l, 0], ffn_out[l, 0], ln_g[l, 0], ln_b[l, 0], alpha, seq)
        if l % 2 == 0:
            e = l // 2
            xa, gg, q, k, v, gs, la = _even_in(h, mod_l, 3, even_w_in[e], gla_wa2[e], gla_ba[e], seq)
            ya = _rglru(xa, gg, conv_w[e], conv_b[e], rg_wa[e], rg_ba[e], rg_wx[e], rg_bx[e], rg_lam[e], seq)
            yb = _gla(q, k, la, v, gs, gla_norm_g[e], seq)
            wo = even_w_out[e].astype(BF16)
            rgw = ya.shape[1]
            h = _proj_deepnorm(h, mod_l, 3, [ya, yb], [wo[:rgw], wo[rgw:]], ln_g[l, 1], ln_b[l, 1], alpha, seq)
        else:
            o = l // 2
            qn, qp, kn, kp, v = _odd_in(h, mod_l, 3, positions, odd_w_in[o], q_norm_g[o], w_q_up[o],
                                        kv_norm_g[o], w_kv_up[o], seq)
            att = _attention(qn, qp, kn, kp, v, seq)
            h = _proj_deepnorm(h, mod_l, 3, [att], [odd_w_out[o].astype(BF16)], ln_g[l, 1], ln_b[l, 1], alpha, seq)
        h = _ffn_sublayer(h, mod_l, 6, ffn_in[l, 1], ffn_out[l, 1], ln_g[l, 2], ln_b[l, 2], alpha, seq)
    return h.reshape(bsz, seq, d)
```

```python
import functools
import math

import jax
import jax.numpy as jnp
from jax import lax
from jax.experimental import pallas as pl
from jax.experimental.pallas import tpu as pltpu

F32 = jnp.float32
BF16 = jnp.bfloat16

LN_EPS = 1e-5
RG_BLOCKS = 16
RG_C = 8.0
CONV_W = 4
GLA_HEADS = 4
GLA_DK = 128
GLA_DV = 256
GLA_RANK = 16
GLA_TAU = 16.0
GLA_CHUNK = 64
MLA_HEADS = 8
MLA_NOPE = 128
MLA_ROPE = 64
MLA_V = 128
MLA_Q_RANK = 256
MLA_KV_RANK = 128
ROPE_THETA = 10000.0
N_MOD = 9

LANES = 128
SUBLANES = 8
VMEM_CAP = 56 << 20
MAX_SAFE_GAP = 96.0

TM_FFN = 1024
TM_PROJ = 512
TM_EVEN_IN = 512
TM_ODD_IN = 512
TC_RGLRU = 512
TC_GLA = 256
TQ_ATTN = 1024
FF_CHUNK = 256


def _params(semantics, vmem_bytes):
    return pltpu.CompilerParams(dimension_semantics=semantics,
                                vmem_limit_bytes=int(min(max(vmem_bytes, 16 << 20), VMEM_CAP)))


def _const_spec(shape):
    nd = len(shape)
    return pl.BlockSpec(shape, lambda *_: (0,) * nd, pipeline_mode=pl.Buffered(1))


def _dot(a, b):
    return jnp.dot(a, b, preferred_element_type=F32)


def _dot_nt(a, b):
    return lax.dot_general(a, b, (((1,), (1,)), ((), ())), preferred_element_type=F32)


def _dot_tn(a, b):
    return lax.dot_general(a, b, (((0,), (0,)), ((), ())), preferred_element_type=F32)


def _sigmoid(x):
    return 1.0 / (1.0 + jnp.exp(-x))


def _silu(x):
    return x * _sigmoid(x)


def _softplus(x):
    return jnp.maximum(x, 0.0) + jnp.log1p(jnp.exp(-jnp.abs(x)))


def _log_sigmoid(x):
    return jnp.minimum(x, 0.0) - jnp.log1p(jnp.exp(-jnp.abs(x)))


def _gelu_tanh(x):
    c = math.sqrt(2.0 / math.pi)
    return x * (0.5 * (1.0 + jnp.tanh(c * (x + 0.044715 * (x * x * x)))))


def _layer_norm(y, g, b):
    mu = jnp.mean(y, axis=-1, keepdims=True)
    d = y - mu
    var = jnp.mean(d * d, axis=-1, keepdims=True)
    return d * lax.rsqrt(var + LN_EPS) * g + b


def _rms_norm(y, g):
    return y * lax.rsqrt(jnp.mean(y * y, axis=-1, keepdims=True) + LN_EPS) * g


def _modulated(x, mod_ref, row):
    return x * (1.0 + mod_ref[row + 1:row + 2, :]) + mod_ref[row:row + 1, :]


def _mod_kernel(c_ref, w_ref, b_ref, o_ref):
    ca = _silu(c_ref[...]).astype(BF16)
    o_ref[...] = _dot(ca, w_ref[...].astype(BF16)) + b_ref[...]


def _mod_vectors(c, mod_w, mod_b):
    depth, d, n = mod_w.shape
    bsz = c.shape[0]
    tn = d
    out = pl.pallas_call(
        _mod_kernel,
        grid=(depth, n // tn),
        in_specs=[pl.BlockSpec((bsz, d), lambda l, j: (0, 0)),
                  pl.BlockSpec((None, d, tn), lambda l, j: (l, 0, j)),
                  pl.BlockSpec((None, 1, tn), lambda l, j: (l, 0, j))],
        out_specs=pl.BlockSpec((None, bsz, tn), lambda l, j: (l, 0, j)),
        out_shape=jax.ShapeDtypeStruct((depth, bsz, n), F32),
        compiler_params=_params(("arbitrary", "arbitrary"), 4 * d * tn * 4),
        name="adaln_mod",
    )(c, mod_w, mod_b.reshape(depth, 1, n))
    return out.reshape(depth, bsz, N_MOD, d)


def _ffn_kernel(x_ref, mod_ref, wg_ref, wu_ref, wo_ref, lng_ref, lnb_ref, o_ref, act_ref,
                *, row, alpha, chunk):
    x = x_ref[...]
    u = _modulated(x, mod_ref, row).astype(BF16)
    d_ff = wg_ref.shape[1]
    for j in range(d_ff // chunk):
        sl = slice(j * chunk, (j + 1) * chunk)
        g = _dot(u, wg_ref[:, sl])
        p = _dot(u, wu_ref[:, sl])
        act_ref[:, sl] = (_silu(g) * p).astype(BF16)
    gate = mod_ref[row + 2:row + 3, :]
    tm = x_ref.shape[0]
    halves = [slice(0, tm // 2), slice(tm // 2, tm)]
    fs = [_dot(act_ref[rs, :], wo_ref[...]) for rs in halves]
    for rs, f in zip(halves, fs):
        y = alpha * x_ref[rs, :] + (0.5 * (1.0 + gate)) * f
        o_ref[rs, :] = _layer_norm(y, lng_ref[...], lnb_ref[...])


def _ffn_sublayer(x, mod_l, row, w_in, w_out, ln_g, ln_b, alpha, seq):
    t, d = x.shape
    d_ff = w_out.shape[0]
    tm = min(TM_FFN, seq)
    tpb = seq // tm
    wg = w_in[:, :d_ff].astype(BF16)
    wu = w_in[:, d_ff:].astype(BF16)
    wo = w_out.astype(BF16)
    vmem = 3 * d * d_ff * 2 + 4 * tm * d * 4 + tm * d_ff * 2 + 6 * tm * d * 4 + (8 << 20)
    return pl.pallas_call(
        functools.partial(_ffn_kernel, row=row, alpha=alpha, chunk=min(FF_CHUNK, d_ff)),
        grid=(t // tm,),
        in_specs=[pl.BlockSpec((tm, d), lambda r: (r, 0)),
                  pl.BlockSpec((None, N_MOD, d), lambda r: (r // tpb, 0, 0)),
                  _const_spec((d, d_ff)), _const_spec((d, d_ff)), _const_spec((d_ff, d)),
                  _const_spec((1, d)), _const_spec((1, d))],
        out_specs=pl.BlockSpec((tm, d), lambda r: (r, 0)),
        out_shape=jax.ShapeDtypeStruct((t, d), F32),
        scratch_shapes=[pltpu.VMEM((tm, d_ff), BF16)],
        compiler_params=_params(("arbitrary",), vmem),
        name="ffn_sublayer",
    )(x, mod_l, wg, wu, wo, ln_g.reshape(1, d), ln_b.reshape(1, d))


def _proj_dn_kernel(*refs, n_in, row, alpha):
    x_ref, mod_ref = refs[0], refs[1]
    a_refs = refs[2:2 + n_in]
    w_refs = refs[2 + n_in:2 + 2 * n_in]
    lng_ref, lnb_ref, o_ref = refs[2 + 2 * n_in:]
    gate = mod_ref[row + 2:row + 3, :]
    tm = x_ref.shape[0]
    halves = [slice(0, tm // 2), slice(tm // 2, tm)]
    ms = []
    for rs in halves:
        m = _dot(a_refs[0][rs, :], w_refs[0][...])
        for a_ref, w_ref in zip(a_refs[1:], w_refs[1:]):
            m = m + _dot(a_ref[rs, :], w_ref[...])
        ms.append(m)
    for rs, m in zip(halves, ms):
        y = alpha * x_ref[rs, :] + (1.0 + gate) * m
        o_ref[rs, :] = _layer_norm(y, lng_ref[...], lnb_ref[...])


def _proj_deepnorm(x, mod_l, row, acts, weights, ln_g, ln_b, alpha, seq):
    t, d = x.shape
    tm = min(TM_PROJ, seq)
    tpb = seq // tm
    n_in = len(acts)
    row_spec = lambda w: pl.BlockSpec((tm, w), lambda r: (r, 0))
    vmem = sum(2 * w.size * 2 for w in weights) + sum(2 * tm * a.shape[1] * 2 for a in acts) + 8 * tm * d * 4 + (4 << 20)
    return pl.pallas_call(
        functools.partial(_proj_dn_kernel, n_in=n_in, row=row, alpha=alpha),
        grid=(t // tm,),
        in_specs=([row_spec(d), pl.BlockSpec((None, N_MOD, d), lambda r: (r // tpb, 0, 0))]
                  + [row_spec(a.shape[1]) for a in acts]
                  + [_const_spec(w.shape) for w in weights]
                  + [_const_spec((1, d)), _const_spec((1, d))]),
        out_specs=row_spec(d),
        out_shape=jax.ShapeDtypeStruct((t, d), F32),
        compiler_params=_params(("arbitrary",), vmem),
        name="proj_deepnorm",
    )(x, mod_l, *acts, *weights, ln_g.reshape(1, d), ln_b.reshape(1, d))


def _even_in_kernel(x_ref, mod_ref, wxa, wga, wq, wk, wv, wg, wza, wa2, ba,
                    xa_o, gg_o, q_o, k_o, v_o, gs_o, la_o, *, row):
    u = _modulated(x_ref[...], mod_ref, row).astype(BF16)
    xa_o[...] = _dot(u, wxa[...])
    gg_o[...] = _gelu_tanh(_dot(u, wga[...]))
    q_o[...] = _dot(u, wq[...]) * (GLA_DK ** -0.5)
    k_o[...] = _dot(u, wk[...])
    v_o[...] = _dot(u, wv[...]).astype(BF16)
    gs_o[...] = _silu(_dot(u, wg[...]))
    za = _dot(u, wza[...]).astype(BF16)
    z = _dot(za, wa2[...]) + ba[...]
    la_o[...] = _log_sigmoid(z) * (1.0 / GLA_TAU)


def _even_in(x, mod_l, row, w_in, gla_wa2, gla_ba, seq):
    t, d = x.shape
    tm = min(TM_EVEN_IN, seq)
    tpb = seq // tm
    hk = GLA_HEADS * GLA_DK
    hv = GLA_HEADS * GLA_DV
    offs = [0, d, 2 * d, 2 * d + hk, 2 * d + 2 * hk, 2 * d + 2 * hk + hv, 2 * d + 2 * hk + 2 * hv]
    wb = w_in.astype(BF16)
    wxa, wga, wq, wk, wv, wg = [wb[:, offs[i]:offs[i + 1]] for i in range(6)]
    wza = jnp.pad(wb[:, offs[6]:], ((0, 0), (0, LANES - GLA_RANK)))
    wa2 = jnp.pad(gla_wa2.astype(BF16), ((0, LANES - GLA_RANK), (0, 0)))
    ws = [wxa, wga, wq, wk, wv, wg, wza, wa2]
    widths = [d, d, hk, hk, hv, hv, hk]
    dtypes = [F32, F32, F32, F32, BF16, F32, F32]
    vmem = sum(w.size * 2 for w in ws) + 2 * tm * sum(widths) * 4 + 4 * tm * d * 4 + (8 << 20)
    return pl.pallas_call(
        functools.partial(_even_in_kernel, row=row),
        grid=(t // tm,),
        in_specs=([pl.BlockSpec((tm, d), lambda r: (r, 0)),
                   pl.BlockSpec((None, N_MOD, d), lambda r: (r // tpb, 0, 0))]
                  + [_const_spec(w.shape) for w in ws] + [_const_spec((1, hk))]),
        out_specs=[pl.BlockSpec((tm, w), lambda r: (r, 0)) for w in widths],
        out_shape=[jax.ShapeDtypeStruct((t, w), dt) for w, dt in zip(widths, dtypes)],
        compiler_params=_params(("arbitrary",), vmem),
        name="even_in_proj",
    )(x, mod_l, *ws, gla_ba.reshape(1, hk))


def _rglru_kernel(xa_ref, gg_ref, cw_ref, cb_ref, wa_ref, wx_ref, ba_ref, bx_ref, lam_ref,
                  o_ref, xpad_ref, a_ref, u_ref, carry_ref, *, tc, groups):
    first = pl.program_id(1) == 0

    @pl.when(first)
    def _():
        xpad_ref[0:SUBLANES, :] = jnp.zeros((SUBLANES, xpad_ref.shape[1]), F32)
        carry_ref[...] = jnp.zeros(carry_ref.shape, F32)

    x = xa_ref[...]
    xpad_ref[SUBLANES:, :] = x
    xc = cw_ref[CONV_W - 1:CONV_W, :] * x + cb_ref[...]
    for dlt in range(1, CONV_W):
        xc = xc + cw_ref[CONV_W - 1 - dlt:CONV_W - dlt, :] * xpad_ref[pl.ds(SUBLANES - dlt, tc), :]
    xpad_ref[0:SUBLANES, :] = x[tc - SUBLANES:, :]

    xb = xc.astype(BF16)
    gw = xb.shape[1] // groups
    neg_c_sp = -RG_C * _softplus(-lam_ref[...])
    for g in range(groups):
        sl = slice(g * gw, (g + 1) * gw)
        r = _sigmoid(_dot(xb[:, sl], wa_ref[g]) + ba_ref[:, sl])
        i = _sigmoid(_dot(xb[:, sl], wx_ref[g]) + bx_ref[:, sl])
        log_a = neg_c_sp[:, sl] * r
        a = jnp.exp(log_a)
        a_ref[:, sl] = a
        z = -jnp.tanh(log_a) * (a * a + 1.0)
        root = jnp.where(z > 0.0, z * lax.rsqrt(z), 0.0)
        u_ref[:, sl] = root * (i * xc[:, sl])

    width = a_ref.shape[1]
    rows = lax.broadcasted_iota(jnp.int32, (SUBLANES, width), 0)

    def body(n, carry):
        r0 = pl.multiple_of(n * SUBLANES, SUBLANES)
        a = a_ref[pl.ds(r0, SUBLANES), :]
        b = u_ref[pl.ds(r0, SUBLANES), :]
        for dlt in (1, 2, 4):
            keep = rows >= dlt
            a_sh = jnp.where(keep, pltpu.roll(a, dlt, 0), 1.0)
            b_sh = jnp.where(keep, pltpu.roll(b, dlt, 0), 0.0)
            b = a * b_sh + b
            a = a * a_sh
        h = b + a * carry
        o_ref[pl.ds(r0, SUBLANES), :] = (h * gg_ref[pl.ds(r0, SUBLANES), :]).astype(o_ref.dtype)
        return jnp.broadcast_to(h[SUBLANES - 1:SUBLANES, :], h.shape)

    carry_ref[...] = lax.fori_loop(0, tc // SUBLANES, body, carry_ref[...], unroll=2)


def _rglru(xa, gg, conv_w, conv_b, rg_wa, rg_ba, rg_wx, rg_bx, rg_lam, seq):
    t, c = xa.shape
    tc = min(TC_RGLRU, seq)
    nt = seq // tc
    bs = c // RG_BLOCKS
    per = (2 * LANES) // bs
    groups = RG_BLOCKS // per

    def block_diag(w):
        w4 = w.reshape(groups, per, bs, bs)
        eye = jnp.eye(per, dtype=w.dtype)
        return jnp.einsum("gaij,ac->gaicj", w4, eye).reshape(groups, per * bs, per * bs).astype(BF16)

    vec = lambda v: v.reshape(1, c)
    tile = pl.BlockSpec((tc, c), lambda b, n: (b * nt + n, 0))
    vmem = 4 * tc * c * 4 + 2 * tc * c * 2 + 3 * tc * c * 4 + 8 * tc * c * 4 + (4 << 20)
    return pl.pallas_call(
        functools.partial(_rglru_kernel, tc=tc, groups=groups),
        grid=(t // seq, nt),
        in_specs=[tile, tile, _const_spec((CONV_W, c)), _const_spec((1, c)),
                  _const_spec((groups, per * bs, per * bs)), _const_spec((groups, per * bs, per * bs)),
                  _const_spec((1, c)), _const_spec((1, c)), _const_spec((1, c))],
        out_specs=tile,
        out_shape=jax.ShapeDtypeStruct((t, c), BF16),
        scratch_shapes=[pltpu.VMEM((tc + SUBLANES, c), F32), pltpu.VMEM((tc, c), F32),
                        pltpu.VMEM((tc, c), F32), pltpu.VMEM((SUBLANES, c), F32)],
        compiler_params=_params(("arbitrary", "arbitrary"), vmem),
        name="rglru",
    )(xa, gg, conv_w, vec(conv_b), block_diag(rg_wa), block_diag(rg_wx), vec(rg_ba), vec(rg_bx), vec(rg_lam))


def _gla_kernel(q_ref, k_ref, la_ref, v_ref, gs_ref, ng_ref, o_ref, st_ref, *, tc):
    @pl.when(pl.program_id(1) == 0)
    def _():
        st_ref[...] = jnp.zeros(st_ref.shape, F32)

    ch = GLA_CHUNK
    nc = tc // ch
    hk = GLA_HEADS * GLA_DK
    ri = lax.broadcasted_iota(jnp.int32, (tc, tc), 0)
    ci = lax.broadcasted_iota(jnp.int32, (tc, tc), 1)
    sh = ch.bit_length() - 1
    same_chunk = jnp.right_shift(ri, sh) == jnp.right_shift(ci, sh)
    tri = jnp.where(same_chunk, jnp.where(ri >= ci, 1.0, 0.0), 0.0).astype(BF16)
    causal = lax.broadcasted_iota(jnp.int32, (ch, ch), 0) >= lax.broadcasted_iota(jnp.int32, (ch, ch), 1)
    ng = ng_ref[...]
    units = [(c, h) for c in range(nc) for h in range(GLA_HEADS)]
    rows = lambda c: slice(c * ch, (c + 1) * ch)
    kcol = lambda h: slice(h * GLA_DK, (h + 1) * GLA_DK)
    vcol = lambda h: slice(h * GLA_DV, (h + 1) * GLA_DV)

    la = la_ref[...]
    hi = la.astype(BF16)
    r1 = la - hi.astype(F32)
    mid = r1.astype(BF16)
    lo = (r1 - mid.astype(F32)).astype(BF16)
    bc3 = _dot(tri, jnp.concatenate([hi, mid, lo], axis=1))
    bc = bc3[:, :hk] + bc3[:, hk:2 * hk] + bc3[:, 2 * hk:]
    kk = k_ref[...]
    qd = (q_ref[...] * jnp.exp(bc)).astype(BF16)
    ki = (kk * jnp.exp(-bc)).astype(BF16)
    b_last = [bc[(c + 1) * ch - 1:(c + 1) * ch, :] for c in range(nc)]
    ke = [(kk[rows(c), :] * jnp.exp(b_last[c] - bc[rows(c), :])).astype(BF16) for c in range(nc)]
    dec = {(c, h): jnp.exp(bc[(c + 1) * ch - 1:(c + 1) * ch, kcol(h)]) for c, h in units}

    s = {(c, h): jnp.where(causal, _dot_nt(qd[rows(c), kcol(h)], ki[rows(c), kcol(h)]), 0.0).astype(BF16)
         for c, h in units}
    kv = {(c, h): _dot_tn(v_ref[rows(c), vcol(h)], ke[c][:, kcol(h)]) for c, h in units}
    st_before = {}
    for h in range(GLA_HEADS):
        st = st_ref[h]
        for c in range(nc):
            st_before[c, h] = st.astype(BF16)
            st = st * dec[c, h] + kv[c, h]
        st_ref[h] = st
    for c, h in units:
        o = _dot(s[c, h], v_ref[rows(c), vcol(h)]) + _dot_nt(qd[rows(c), kcol(h)], st_before[c, h])
        o_ref[rows(c), vcol(h)] = (_rms_norm(o, ng) * gs_ref[rows(c), vcol(h)]).astype(o_ref.dtype)


def _gla(q, k, la, v, gs, norm_g, seq):
    t = q.shape[0]
    tc = min(TC_GLA, seq)
    nt = seq // tc
    hk = GLA_HEADS * GLA_DK
    hv = GLA_HEADS * GLA_DV
    tile_k = pl.BlockSpec((tc, hk), lambda b, n: (b * nt + n, 0))
    tile_v = pl.BlockSpec((tc, hv), lambda b, n: (b * nt + n, 0))
    vmem = 2 * tc * (3 * hk * 4 + hv * 2 + hv * 4 + hv * 2) + GLA_HEADS * GLA_DV * GLA_DK * 4 + (16 << 20)
    return pl.pallas_call(
        functools.partial(_gla_kernel, tc=tc),
        grid=(t // seq, nt),
        in_specs=[tile_k, tile_k, tile_k, tile_v, tile_v, _const_spec((1, GLA_DV))],
        out_specs=tile_v,
        out_shape=jax.ShapeDtypeStruct((t, hv), BF16),
        scratch_shapes=[pltpu.VMEM((GLA_HEADS, GLA_DV, GLA_DK), F32)],
        compiler_params=_params(("arbitrary", "arbitrary"), vmem),
        name="gla",
    )(q, k, la, v, gs, norm_g.reshape(1, GLA_DV))


def _odd_in_kernel(x_ref, mod_ref, pos_ref, freq_ref, sgn_ref, wcq, wckv, wkp, wkps, qg, kvg,
                   wqn, wqp, wqps, wkn, wvt, qn_o, qp_o, kn_o, kp_o, vt_o, *, row, scale):
    u = _modulated(x_ref[...], mod_ref, row).astype(BF16)
    cqn = _rms_norm(_dot(u, wcq[...]), qg[...]).astype(BF16)
    ckvn = _rms_norm(_dot(u, wckv[...]), kvg[...]).astype(BF16)
    ang = pos_ref[...].astype(F32) * freq_ref[...]
    cos = jnp.cos(ang)
    sin = jnp.sin(ang) * sgn_ref[...]
    kp_o[...] = (_dot(u, wkp[...]) * cos + _dot(u, wkps[...]) * sin).astype(BF16)
    cos_h = jnp.concatenate([cos] * MLA_HEADS, axis=1)
    sin_h = jnp.concatenate([sin] * MLA_HEADS, axis=1)
    qn_o[...] = (_dot(cqn, wqn[...]) * scale).astype(BF16)
    qp_o[...] = ((_dot(cqn, wqp[...]) * cos_h + _dot(cqn, wqps[...]) * sin_h) * scale).astype(BF16)
    kn_o[...] = _dot(ckvn, wkn[...]).astype(BF16)
    vt_o[...] = _dot_nt(wvt[...], ckvn).astype(BF16)


def _swap_halves(w):
    half = w.shape[-1] // 2
    return jnp.concatenate([w[..., half:], w[..., :half]], axis=-1)


def _odd_in(x, mod_l, row, positions, w_in, q_norm_g, w_q_up, kv_norm_g, w_kv_up, seq):
    t, d = x.shape
    tm = min(TM_ODD_IN, seq)
    tpb = seq // tm
    hn = MLA_HEADS * MLA_NOPE
    hv = MLA_HEADS * MLA_V
    pad_r = LANES - MLA_ROPE
    wb = w_in.astype(BF16)
    wcq = wb[:, :MLA_Q_RANK]
    wckv = wb[:, MLA_Q_RANK:MLA_Q_RANK + MLA_KV_RANK]
    wkpe = wb[:, MLA_Q_RANK + MLA_KV_RANK:]
    wkp = jnp.pad(wkpe, ((0, 0), (0, pad_r)))
    wkps = jnp.pad(_swap_halves(wkpe), ((0, 0), (0, pad_r)))
    wq = w_q_up.astype(BF16).reshape(MLA_Q_RANK, MLA_HEADS, MLA_NOPE + MLA_ROPE)
    wqn = wq[:, :, :MLA_NOPE].reshape(MLA_Q_RANK, hn)
    wq_pe = wq[:, :, MLA_NOPE:]
    pad3 = ((0, 0), (0, 0), (0, pad_r))
    wqp = jnp.pad(wq_pe, pad3).reshape(MLA_Q_RANK, MLA_HEADS * LANES)
    wqps = jnp.pad(_swap_halves(wq_pe), pad3).reshape(MLA_Q_RANK, MLA_HEADS * LANES)
    wkv = w_kv_up.astype(BF16).reshape(MLA_KV_RANK, MLA_HEADS, MLA_NOPE + MLA_V)
    wkn = wkv[:, :, :MLA_NOPE].reshape(MLA_KV_RANK, hn)
    wvt = wkv[:, :, MLA_NOPE:].reshape(MLA_KV_RANK, hv).T
    half = MLA_ROPE // 2
    freqs = ROPE_THETA ** (-jnp.arange(half, dtype=F32) / half)
    zeros = jnp.zeros((pad_r,), F32)
    freq_row = jnp.concatenate([freqs, freqs, zeros]).reshape(1, LANES)
    sgn_row = jnp.concatenate([-jnp.ones((half,), F32), jnp.ones((half,), F32), zeros]).reshape(1, LANES)
    ws = [wcq, wckv, wkp, wkps, q_norm_g.reshape(1, -1), kv_norm_g.reshape(1, -1), wqn, wqp, wqps, wkn, wvt]
    widths = [hn, MLA_HEADS * LANES, hn, LANES]
    scale = math.log2(math.e) * (MLA_NOPE + MLA_ROPE) ** -0.5
    vmem = 2 * sum(w.size * 2 for w in ws) + 2 * tm * sum(widths) * 2 + 16 * tm * d * 4 + (8 << 20)
    return pl.pallas_call(
        functools.partial(_odd_in_kernel, row=row, scale=scale),
        grid=(t // tm,),
        in_specs=([pl.BlockSpec((tm, d), lambda r: (r, 0)),
                   pl.BlockSpec((None, N_MOD, d), lambda r: (r // tpb, 0, 0)),
                   pl.BlockSpec((tm, 1), lambda r: (r, 0)),
                   _const_spec((1, LANES)), _const_spec((1, LANES))]
                  + [_const_spec(w.shape) for w in ws]),
        out_specs=([pl.BlockSpec((tm, w), lambda r: (r, 0)) for w in widths]
                   + [pl.BlockSpec((hv, tm), lambda r: (0, r))]),
        out_shape=([jax.ShapeDtypeStruct((t, w), BF16) for w in widths]
                   + [jax.ShapeDtypeStruct((hv, t), BF16)]),
        compiler_params=_params(("arbitrary",), vmem),
        name="odd_in_proj",
    )(x, mod_l, positions.reshape(t, 1), freq_row, sgn_row, *ws)


def _attn_kernel(qn_ref, qp_ref, qn_next_ref, qp_next_ref, kn_ref, kp_ref, vt_ref, o_ref,
                 s_ref, mx_ref, p_ref, acc_ref, gap_ref, *, tq):
    i = pl.program_id(2)
    q = jnp.concatenate([qn_ref[...], qp_ref[...]], axis=1)
    FIRST = 2

    def put_scores(t, slot, qmat):
        k0 = pl.multiple_of(t * tq, tq)
        kt = jnp.concatenate([kn_ref[pl.ds(k0, tq), :], kp_ref[pl.ds(k0, tq), :]], axis=1)
        st = _dot_nt(kt, qmat)
        s_ref[slot] = st
        mx_ref[slot] = jnp.broadcast_to(jnp.max(st, axis=0, keepdims=True), (SUBLANES, tq))

    def pv(t, slot):
        k0 = pl.multiple_of(t * tq, tq)
        return _dot(vt_ref[:, pl.ds(k0, tq)], p_ref[slot])

    def causal_mask(st):
        key = lax.broadcasted_iota(jnp.int32, st.shape, 0)
        qry = lax.broadcasted_iota(jnp.int32, st.shape, 1)
        return jnp.where(key <= qry, st, -jnp.inf)

    def prefetch_next_block():
        q_next = jnp.concatenate([qn_next_ref[...], qp_next_ref[...]], axis=1)
        put_scores(0, FIRST, q_next)

    def first_tile(last):
        st = s_ref[FIRST]
        if last:
            st = causal_mask(st)
            m0 = jnp.max(st, axis=0, keepdims=True)
        else:
            m0 = mx_ref[FIRST][0:1, :]
        p = jnp.exp2(st - m0)
        l0 = jnp.sum(p, axis=0, keepdims=True)
        p_ref[0] = p.astype(BF16)
        if last:
            prefetch_next_block()
            o_ref[...] = (pv(0, 0) / l0).T.astype(o_ref.dtype)
            gap_ref[...] = jnp.zeros(gap_ref.shape, F32)
            return None
        acc_ref[...] = jnp.zeros(acc_ref.shape, F32)
        return m0, jnp.ones((1, tq), F32), l0, jnp.zeros((1, tq), F32)

    def fast_step(t, p_in, p_out, carry, last):
        m_prev, a_prev, l_prev, gap = carry
        k0 = pl.multiple_of(t * tq, tq)
        kt = jnp.concatenate([kn_ref[pl.ds(k0, tq), :], kp_ref[pl.ds(k0, tq), :]], axis=1)
        st = _dot_nt(kt, q)
        if last:
            st = causal_mask(st)
        cmax = jnp.max(st, axis=0, keepdims=True)
        p = jnp.exp2(st - m_prev)
        lsum = jnp.sum(p, axis=0, keepdims=True)
        p_ref[p_out] = p.astype(BF16)
        gap = jnp.maximum(gap, cmax - m_prev)
        acc = (acc_ref[...] + pv(t - 1, p_in)) * a_prev
        if last:
            prefetch_next_block()
            acc = acc + pv(t, p_out)
            o_ref[...] = (acc / (l_prev + lsum)).T.astype(o_ref.dtype)
            gap_ref[...] = jnp.broadcast_to(gap, gap_ref.shape)
            return None
        acc_ref[...] = acc
        m_new = jnp.maximum(m_prev, cmax)
        alpha = jnp.exp2(m_prev - m_new)
        return m_new, alpha, (l_prev + lsum) * alpha, gap

    @pl.when(i == 0)
    def _():
        put_scores(0, FIRST, q)
        first_tile(True)

    @pl.when(i > 0)
    def _():
        carry = first_tile(False)

        def pair(n, carry):
            carry = fast_step(2 * n + 1, 0, 1, carry, False)
            return fast_step(2 * n + 2, 1, 0, carry, False)

        rest = i - 1
        carry = lax.fori_loop(0, lax.shift_right_logical(rest, 1), pair, carry)
        odd = (rest & 1) == 1

        @pl.when(odd)
        def _():
            fast_step(i, 1, 0, fast_step(i - 1, 0, 1, carry, False), True)

        @pl.when(jnp.logical_not(odd))
        def _():
            fast_step(i, 0, 1, carry, True)

    def robust_step(t, slot, carry, last):
        m_prev, l_prev = carry
        st = s_ref[slot]
        if last:
            st = causal_mask(st)
            m_cur = jnp.max(st, axis=0, keepdims=True)
        else:
            m_cur = mx_ref[slot][0:1, :]
        m_new = jnp.maximum(m_prev, m_cur)
        alpha = jnp.exp2(m_prev - m_new)
        p = jnp.exp2(st - m_new)
        l_new = alpha * l_prev + jnp.sum(p, axis=0, keepdims=True)
        p_ref[slot] = p.astype(BF16)
        if not last:
            put_scores(t + 1, 1 - slot, q)
        acc = alpha * (acc_ref[...] + pv(jnp.maximum(t - 1, 0), 1 - slot))
        if last:
            acc = acc + pv(t, slot)
            o_ref[...] = (acc / l_new).T.astype(o_ref.dtype)
        else:
            acc_ref[...] = acc
        return m_new, l_new

    @pl.when(jnp.max(gap_ref[0:1, :]) > MAX_SAFE_GAP)
    def _():
        put_scores(0, 0, q)
        p_ref[1] = jnp.zeros(p_ref.shape[1:], BF16)
        acc_ref[...] = jnp.zeros(acc_ref.shape, F32)
        carry = (jnp.full((1, tq), -jnp.inf, F32), jnp.zeros((1, tq), F32))

        def pair(n, carry):
            return robust_step(2 * n + 1, 1, robust_step(2 * n, 0, carry, False), False)

        carry = lax.fori_loop(0, lax.shift_right_logical(i, 1), pair, carry)
        odd = (i & 1) == 1

        @pl.when(odd)
        def _():
            robust_step(i, 1, robust_step(i - 1, 0, carry, False), True)

        @pl.when(jnp.logical_not(odd))
        def _():
            robust_step(i, 0, carry, True)


def _attention(qn, qp, kn, kp, vt, seq):
    t = qn.shape[0]
    bsz = t // seq
    tq = min(TQ_ATTN, seq)
    nq = seq // tq
    q_spec = pl.BlockSpec((tq, LANES), lambda b, h, i: (b * nq + i, h))
    kn_spec = pl.BlockSpec((seq, LANES), lambda b, h, i: (b, h))
    kp_spec = pl.BlockSpec((seq, LANES), lambda b, h, i: (b, 0))
    vt_spec = pl.BlockSpec((LANES, seq), lambda b, h, i: (h, b))
    q_next_spec = pl.BlockSpec((tq, LANES), lambda b, h, i: (b * nq + jnp.minimum(i + 1, nq - 1), h))
    vmem = 2 * 3 * seq * LANES * 2 + 10 * tq * LANES * 2 + 3 * tq * tq * 4 + 2 * tq * tq * 2 + 8 * tq * LANES * 4 + (8 << 20)
    return pl.pallas_call(
        functools.partial(_attn_kernel, tq=tq),
        grid=(bsz, MLA_HEADS, nq),
        in_specs=[q_spec, q_spec, q_next_spec, q_next_spec, kn_spec, kp_spec, vt_spec],
        out_specs=q_spec,
        out_shape=jax.ShapeDtypeStruct((t, MLA_HEADS * MLA_V), BF16),
        scratch_shapes=[pltpu.VMEM((3, tq, tq), F32), pltpu.VMEM((3, SUBLANES, tq), F32),
                        pltpu.VMEM((2, tq, tq), BF16), pltpu.VMEM((MLA_V, tq), F32),
                        pltpu.VMEM((SUBLANES, tq), F32)],
        compiler_params=_params(("arbitrary", "arbitrary", "arbitrary"), vmem),
        name="mla_attention",
    )(qn, qp, qn, qp, kn, kp, vt)


def kernel(x, c, positions, mod_w, mod_b, ffn_in, ffn_out, ln_g, ln_b, even_w_in, conv_w, conv_b, rg_wa, rg_ba, rg_wx, rg_bx, rg_lam, gla_wa2, gla_ba, gla_norm_g, even_w_out, odd_w_in, q_norm_g, w_q_up, kv_norm_g, w_kv_up, odd_w_out):
    bsz, seq, d = x.shape
    depth = mod_w.shape[0]
    alpha = (2.0 * depth) ** 0.25
    mod = _mod_vectors(c, mod_w, mod_b)
    h = x.reshape(bsz * seq, d)
    for l in range(depth):
        mod_l = mod[l]
        h = _ffn_sublayer(h, mod_l, 0, ffn_in[l, 0], ffn_out[l, 0], ln_g[l, 0], ln_b[l, 0], alpha, seq)
        if l % 2 == 0:
            e = l // 2
            xa, gg, q, k, v, gs, la = _even_in(h, mod_l, 3, even_w_in[e], gla_wa2[e], gla_ba[e], seq)
            ya = _rglru(xa, gg, conv_w[e], conv_b[e], rg_wa[e], rg_ba[e], rg_wx[e], rg_bx[e], rg_lam[e], seq)
            yb = _gla(q, k, la, v, gs, gla_norm_g[e], seq)
            wo = even_w_out[e].astype(BF16)
            rgw = ya.shape[1]
            h = _proj_deepnorm(h, mod_l, 3, [ya, yb], [wo[:rgw], wo[rgw:]], ln_g[l, 1], ln_b[l, 1], alpha, seq)
        else:
            o = l // 2
            qn, qp, kn, kp, v = _odd_in(h, mod_l, 3, positions, odd_w_in[o], q_norm_g[o], w_q_up[o],
                                        kv_norm_g[o], w_kv_up[o], seq)
            att = _attention(qn, qp, kn, kp, v, seq)
            h = _proj_deepnorm(h, mod_l, 3, [att], [odd_w_out[o].astype(BF16)], ln_g[l, 1], ln_b[l, 1], alpha, seq)
        h = _ffn_sublayer(h, mod_l, 6, ffn_in[l, 1], ffn_out[l, 1], ln_g[l, 2], ln_b[l, 2], alpha, seq)
    return h.reshape(bsz, seq, d)
```

```python
import functools
import math

import jax
import jax.numpy as jnp
from jax import lax
from jax.experimental import pallas as pl
from jax.experimental.pallas import tpu as pltpu

F32 = jnp.float32
BF16 = jnp.bfloat16

LN_EPS = 1e-5
RG_BLOCKS = 16
RG_C = 8.0
CONV_W = 4
GLA_HEADS = 4
GLA_DK = 128
GLA_DV = 256
GLA_RANK = 16
GLA_TAU = 16.0
GLA_CHUNK = 64
MLA_HEADS = 8
MLA_NOPE = 128
MLA_ROPE = 64
MLA_V = 128
MLA_Q_RANK = 256
MLA_KV_RANK = 128
ROPE_THETA = 10000.0
N_MOD = 9

LANES = 128
SUBLANES = 8
VMEM_CAP = 56 << 20
MAX_SAFE_GAP = 96.0

TM_FFN = 1024
TM_PROJ = 512
TM_EVEN_IN = 512
TM_ODD_IN = 512
TC_RGLRU = 512
TC_GLA = 256
TQ_ATTN = 1024
FF_CHUNK = 256


def _params(semantics, vmem_bytes):
    return pltpu.CompilerParams(dimension_semantics=semantics,
                                vmem_limit_bytes=int(min(max(vmem_bytes, 16 << 20), VMEM_CAP)))


def _const_spec(shape):
    nd = len(shape)
    return pl.BlockSpec(shape, lambda *_: (0,) * nd, pipeline_mode=pl.Buffered(1))


def _dot(a, b):
    return jnp.dot(a, b, preferred_element_type=F32)


def _dot_nt(a, b):
    return lax.dot_general(a, b, (((1,), (1,)), ((), ())), preferred_element_type=F32)


def _dot_tn(a, b):
    return lax.dot_general(a, b, (((0,), (0,)), ((), ())), preferred_element_type=F32)


def _sigmoid(x):
    return 1.0 / (1.0 + jnp.exp(-x))


def _silu(x):
    return x * _sigmoid(x)


def _softplus(x):
    return jnp.maximum(x, 0.0) + jnp.log1p(jnp.exp(-jnp.abs(x)))


def _log_sigmoid(x):
    return jnp.minimum(x, 0.0) - jnp.log1p(jnp.exp(-jnp.abs(x)))


def _gelu_tanh(x):
    c = math.sqrt(2.0 / math.pi)
    return x * (0.5 * (1.0 + jnp.tanh(c * (x + 0.044715 * (x * x * x)))))


def _layer_norm(y, g, b):
    mu = jnp.mean(y, axis=-1, keepdims=True)
    d = y - mu
    var = jnp.mean(d * d, axis=-1, keepdims=True)
    return d * lax.rsqrt(var + LN_EPS) * g + b


def _rms_norm(y, g):
    return y * lax.rsqrt(jnp.mean(y * y, axis=-1, keepdims=True) + LN_EPS) * g


def _modulated(x, mod_ref, row):
    return x * (1.0 + mod_ref[row + 1:row + 2, :]) + mod_ref[row:row + 1, :]


def _mod_kernel(c_ref, w_ref, b_ref, o_ref):
    ca = _silu(c_ref[...]).astype(BF16)
    o_ref[...] = _dot(ca, w_ref[...].astype(BF16)) + b_ref[...]


def _mod_vectors(c, mod_w, mod_b):
    depth, d, n = mod_w.shape
    bsz = c.shape[0]
    tn = d
    out = pl.pallas_call(
        _mod_kernel,
        grid=(depth, n // tn),
        in_specs=[pl.BlockSpec((bsz, d), lambda l, j: (0, 0)),
                  pl.BlockSpec((None, d, tn), lambda l, j: (l, 0, j)),
                  pl.BlockSpec((None, 1, tn), lambda l, j: (l, 0, j))],
        out_specs=pl.BlockSpec((None, bsz, tn), lambda l, j: (l, 0, j)),
        out_shape=jax.ShapeDtypeStruct((depth, bsz, n), F32),
        compiler_params=_params(("arbitrary", "arbitrary"), 4 * d * tn * 4),
        name="adaln_mod",
    )(c, mod_w, mod_b.reshape(depth, 1, n))
    return out.reshape(depth, bsz, N_MOD, d)


def _ffn_kernel(x_ref, mod_ref, wg_ref, wu_ref, wo_ref, lng_ref, lnb_ref, o_ref, act_ref,
                *, row, alpha, chunk):
    x = x_ref[...]
    u = _modulated(x, mod_ref, row).astype(BF16)
    d_ff = wg_ref.shape[1]
    for j in range(d_ff // chunk):
        sl = slice(j * chunk, (j + 1) * chunk)
        g = _dot(u, wg_ref[:, sl])
        p = _dot(u, wu_ref[:, sl])
        act_ref[:, sl] = (_silu(g) * p).astype(BF16)
    gate = mod_ref[row + 2:row + 3, :]
    tm = x_ref.shape[0]
    halves = [slice(0, tm // 2), slice(tm // 2, tm)]
    fs = [_dot(act_ref[rs, :], wo_ref[...]) for rs in halves]
    for rs, f in zip(halves, fs):
        y = alpha * x_ref[rs, :] + (0.5 * (1.0 + gate)) * f
        o_ref[rs, :] = _layer_norm(y, lng_ref[...], lnb_ref[...])


def _ffn_sublayer(x, mod_l, row, w_in, w_out, ln_g, ln_b, alpha, seq):
    t, d = x.shape
    d_ff = w_out.shape[0]
    tm = min(TM_FFN, seq)
    tpb = seq // tm
    wg = w_in[:, :d_ff].astype(BF16)
    wu = w_in[:, d_ff:].astype(BF16)
    wo = w_out.astype(BF16)
    vmem = 3 * d * d_ff * 2 + 4 * tm * d * 4 + tm * d_ff * 2 + 6 * tm * d * 4 + (8 << 20)
    return pl.pallas_call(
        functools.partial(_ffn_kernel, row=row, alpha=alpha, chunk=min(FF_CHUNK, d_ff)),
        grid=(t // tm,),
        in_specs=[pl.BlockSpec((tm, d), lambda r: (r, 0)),
                  pl.BlockSpec((None, N_MOD, d), lambda r: (r // tpb, 0, 0)),
                  _const_spec((d, d_ff)), _const_spec((d, d_ff)), _const_spec((d_ff, d)),
                  _const_spec((1, d)), _const_spec((1, d))],
        out_specs=pl.BlockSpec((tm, d), lambda r: (r, 0)),
        out_shape=jax.ShapeDtypeStruct((t, d), F32),
        scratch_shapes=[pltpu.VMEM((tm, d_ff), BF16)],
        compiler_params=_params(("arbitrary",), vmem),
        name="ffn_sublayer",
    )(x, mod_l, wg, wu, wo, ln_g.reshape(1, d), ln_b.reshape(1, d))


def _proj_dn_kernel(*refs, n_in, row, alpha):
    x_ref, mod_ref = refs[0], refs[1]
    a_refs = refs[2:2 + n_in]
    w_refs = refs[2 + n_in:2 + 2 * n_in]
    lng_ref, lnb_ref, o_ref = refs[2 + 2 * n_in:]
    gate = mod_ref[row + 2:row + 3, :]
    tm = x_ref.shape[0]
    halves = [slice(0, tm // 2), slice(tm // 2, tm)]
    ms = []
    for rs in halves:
        m = _dot(a_refs[0][rs, :], w_refs[0][...])
        for a_ref, w_ref in zip(a_refs[1:], w_refs[1:]):
            m = m + _dot(a_ref[rs, :], w_ref[...])
        ms.append(m)
    for rs, m in zip(halves, ms):
        y = alpha * x_ref[rs, :] + (1.0 + gate) * m
        o_ref[rs, :] = _layer_norm(y, lng_ref[...], lnb_ref[...])


def _proj_deepnorm(x, mod_l, row, acts, weights, ln_g, ln_b, alpha, seq):
    t, d = x.shape
    tm = min(TM_PROJ, seq)
    tpb = seq // tm
    n_in = len(acts)
    row_spec = lambda w: pl.BlockSpec((tm, w), lambda r: (r, 0))
    vmem = sum(2 * w.size * 2 for w in weights) + sum(2 * tm * a.shape[1] * 2 for a in acts) + 8 * tm * d * 4 + (4 << 20)
    return pl.pallas_call(
        functools.partial(_proj_dn_kernel, n_in=n_in, row=row, alpha=alpha),
        grid=(t // tm,),
        in_specs=([row_spec(d), pl.BlockSpec((None, N_MOD, d), lambda r: (r // tpb, 0, 0))]
                  + [row_spec(a.shape[1]) for a in acts]
                  + [_const_spec(w.shape) for w in weights]
                  + [_const_spec((1, d)), _const_spec((1, d))]),
        out_specs=row_spec(d),
        out_shape=jax.ShapeDtypeStruct((t, d), F32),
        compiler_params=_params(("arbitrary",), vmem),
        name="proj_deepnorm",
    )(x, mod_l, *acts, *weights, ln_g.reshape(1, d), ln_b.reshape(1, d))


def _even_in_kernel(x_ref, mod_ref, wxa, wga, wq, wk, wv, wg, wza, wa2, ba,
                    xa_o, gg_o, q_o, k_o, v_o, gs_o, la_o, *, row):
    u = _modulated(x_ref[...], mod_ref, row).astype(BF16)
    xa_o[...] = _dot(u, wxa[...])
    gg_o[...] = _gelu_tanh(_dot(u, wga[...]))
    q_o[...] = _dot(u, wq[...]) * (GLA_DK ** -0.5)
    k_o[...] = _dot(u, wk[...])
    v_o[...] = _dot(u, wv[...]).astype(BF16)
    gs_o[...] = _silu(_dot(u, wg[...]))
    za = _dot(u, wza[...]).astype(BF16)
    z = _dot(za, wa2[...]) + ba[...]
    la_o[...] = _log_sigmoid(z) * (1.0 / GLA_TAU)


def _even_in(x, mod_l, row, w_in, gla_wa2, gla_ba, seq):
    t, d = x.shape
    tm = min(TM_EVEN_IN, seq)
    tpb = seq // tm
    hk = GLA_HEADS * GLA_DK
    hv = GLA_HEADS * GLA_DV
    offs = [0, d, 2 * d, 2 * d + hk, 2 * d + 2 * hk, 2 * d + 2 * hk + hv, 2 * d + 2 * hk + 2 * hv]
    wb = w_in.astype(BF16)
    wxa, wga, wq, wk, wv, wg = [wb[:, offs[i]:offs[i + 1]] for i in range(6)]
    wza = jnp.pad(wb[:, offs[6]:], ((0, 0), (0, LANES - GLA_RANK)))
    wa2 = jnp.pad(gla_wa2.astype(BF16), ((0, LANES - GLA_RANK), (0, 0)))
    ws = [wxa, wga, wq, wk, wv, wg, wza, wa2]
    widths = [d, d, hk, hk, hv, hv, hk]
    dtypes = [F32, F32, F32, F32, BF16, F32, F32]
    vmem = sum(w.size * 2 for w in ws) + 2 * tm * sum(widths) * 4 + 4 * tm * d * 4 + (8 << 20)
    return pl.pallas_call(
        functools.partial(_even_in_kernel, row=row),
        grid=(t // tm,),
        in_specs=([pl.BlockSpec((tm, d), lambda r: (r, 0)),
                   pl.BlockSpec((None, N_MOD, d), lambda r: (r // tpb, 0, 0))]
                  + [_const_spec(w.shape) for w in ws] + [_const_spec((1, hk))]),
        out_specs=[pl.BlockSpec((tm, w), lambda r: (r, 0)) for w in widths],
        out_shape=[jax.ShapeDtypeStruct((t, w), dt) for w, dt in zip(widths, dtypes)],
        compiler_params=_params(("arbitrary",), vmem),
        name="even_in_proj",
    )(x, mod_l, *ws, gla_ba.reshape(1, hk))


def _rglru_kernel(xa_ref, gg_ref, cw_ref, cb_ref, wa_ref, wx_ref, ba_ref, bx_ref, lam_ref,
                  o_ref, xpad_ref, a_ref, u_ref, carry_ref, *, tc, groups):
    first = pl.program_id(1) == 0

    @pl.when(first)
    def _():
        xpad_ref[0:SUBLANES, :] = jnp.zeros((SUBLANES, xpad_ref.shape[1]), F32)
        carry_ref[...] = jnp.zeros(carry_ref.shape, F32)

    x = xa_ref[...]
    xpad_ref[SUBLANES:, :] = x
    xc = cw_ref[CONV_W - 1:CONV_W, :] * x + cb_ref[...]
    for dlt in range(1, CONV_W):
        xc = xc + cw_ref[CONV_W - 1 - dlt:CONV_W - dlt, :] * xpad_ref[pl.ds(SUBLANES - dlt, tc), :]
    xpad_ref[0:SUBLANES, :] = x[tc - SUBLANES:, :]

    xb = xc.astype(BF16)
    gw = xb.shape[1] // groups
    neg_c_sp = -RG_C * _softplus(-lam_ref[...])
    for g in range(groups):
        sl = slice(g * gw, (g + 1) * gw)
        r = _sigmoid(_dot(xb[:, sl], wa_ref[g]) + ba_ref[:, sl])
        i = _sigmoid(_dot(xb[:, sl], wx_ref[g]) + bx_ref[:, sl])
        log_a = neg_c_sp[:, sl] * r
        a = jnp.exp(log_a)
        a_ref[:, sl] = a
        z = -jnp.tanh(log_a) * (a * a + 1.0)
        root = jnp.where(z > 0.0, z * lax.rsqrt(z), 0.0)
        u_ref[:, sl] = root * (i * xc[:, sl])

    width = a_ref.shape[1]
    rows = lax.broadcasted_iota(jnp.int32, (SUBLANES, width), 0)

    def body(n, carry):
        r0 = pl.multiple_of(n * SUBLANES, SUBLANES)
        a = a_ref[pl.ds(r0, SUBLANES), :]
        b = u_ref[pl.ds(r0, SUBLANES), :]
        for dlt in (1, 2, 4):
            keep = rows >= dlt
            a_sh = jnp.where(keep, pltpu.roll(a, dlt, 0), 1.0)
            b_sh = jnp.where(keep, pltpu.roll(b, dlt, 0), 0.0)
            b = a * b_sh + b
            a = a * a_sh
        h = b + a * carry
        o_ref[pl.ds(r0, SUBLANES), :] = (h * gg_ref[pl.ds(r0, SUBLANES), :]).astype(o_ref.dtype)
        return jnp.broadcast_to(h[SUBLANES - 1:SUBLANES, :], h.shape)

    carry_ref[...] = lax.fori_loop(0, tc // SUBLANES, body, carry_ref[...], unroll=2)


def _rglru(xa, gg, conv_w, conv_b, rg_wa, rg_ba, rg_wx, rg_bx, rg_lam, seq):
    t, c = xa.shape
    tc = min(TC_RGLRU, seq)
    nt = seq // tc
    bs = c // RG_BLOCKS
    per = (2 * LANES) // bs
    groups = RG_BLOCKS // per

    def block_diag(w):
        w4 = w.reshape(groups, per, bs, bs)
        eye = jnp.eye(per, dtype=w.dtype)
        return jnp.einsum("gaij,ac->gaicj", w4, eye).reshape(groups, per * bs, per * bs).astype(BF16)

    vec = lambda v: v.reshape(1, c)
    tile = pl.BlockSpec((tc, c), lambda b, n: (b * nt + n, 0))
    vmem = 4 * tc * c * 4 + 2 * tc * c * 2 + 3 * tc * c * 4 + 8 * tc * c * 4 + (4 << 20)
    return pl.pallas_call(
        functools.partial(_rglru_kernel, tc=tc, groups=groups),
        grid=(t // seq, nt),
        in_specs=[tile, tile, _const_spec((CONV_W, c)), _const_spec((1, c)),
                  _const_spec((groups, per * bs, per * bs)), _const_spec((groups, per * bs, per * bs)),
                  _const_spec((1, c)), _const_spec((1, c)), _const_spec((1, c))],
        out_specs=tile,
        out_shape=jax.ShapeDtypeStruct((t, c), BF16),
        scratch_shapes=[pltpu.VMEM((tc + SUBLANES, c), F32), pltpu.VMEM((tc, c), F32),
                        pltpu.VMEM((tc, c), F32), pltpu.VMEM((SUBLANES, c), F32)],
        compiler_params=_params(("arbitrary", "arbitrary"), vmem),
        name="rglru",
    )(xa, gg, conv_w, vec(conv_b), block_diag(rg_wa), block_diag(rg_wx), vec(rg_ba), vec(rg_bx), vec(rg_lam))


def _gla_kernel(q_ref, k_ref, la_ref, v_ref, gs_ref, ng_ref, o_ref, st_ref, *, tc):
    @pl.when(pl.program_id(1) == 0)
    def _():
        st_ref[...] = jnp.zeros(st_ref.shape, F32)

    ch = GLA_CHUNK
    nc = tc // ch
    hk = GLA_HEADS * GLA_DK
    ri = lax.broadcasted_iota(jnp.int32, (tc, tc), 0)
    ci = lax.broadcasted_iota(jnp.int32, (tc, tc), 1)
    sh = ch.bit_length() - 1
    same_chunk = jnp.right_shift(ri, sh) == jnp.right_shift(ci, sh)
    tri = jnp.where(same_chunk, jnp.where(ri >= ci, 1.0, 0.0), 0.0).astype(BF16)
    causal = lax.broadcasted_iota(jnp.int32, (ch, ch), 0) >= lax.broadcasted_iota(jnp.int32, (ch, ch), 1)
    ng = ng_ref[...]
    units = [(c, h) for c in range(nc) for h in range(GLA_HEADS)]
    rows = lambda c: slice(c * ch, (c + 1) * ch)
    kcol = lambda h: slice(h * GLA_DK, (h + 1) * GLA_DK)
    vcol = lambda h: slice(h * GLA_DV, (h + 1) * GLA_DV)

    la = la_ref[...]
    hi = la.astype(BF16)
    r1 = la - hi.astype(F32)
    mid = r1.astype(BF16)
    lo = (r1 - mid.astype(F32)).astype(BF16)
    bc3 = _dot(tri, jnp.concatenate([hi, mid, lo], axis=1))
    bc = bc3[:, :hk] + bc3[:, hk:2 * hk] + bc3[:, 2 * hk:]
    kk = k_ref[...]
    qd = (q_ref[...] * jnp.exp(bc)).astype(BF16)
    ki = (kk * jnp.exp(-bc)).astype(BF16)
    b_last = [bc[(c + 1) * ch - 1:(c + 1) * ch, :] for c in range(nc)]
    ke = [(kk[rows(c), :] * jnp.exp(b_last[c] - bc[rows(c), :])).astype(BF16) for c in range(nc)]
    dec = {(c, h): jnp.exp(bc[(c + 1) * ch - 1:(c + 1) * ch, kcol(h)]) for c, h in units}

    s = {(c, h): jnp.where(causal, _dot_nt(qd[rows(c), kcol(h)], ki[rows(c), kcol(h)]), 0.0).astype(BF16)
         for c, h in units}
    kv = {(c, h): _dot_tn(v_ref[rows(c), vcol(h)], ke[c][:, kcol(h)]) for c, h in units}
    st_before = {}
    for h in range(GLA_HEADS):
        st = st_ref[h]
        for c in range(nc):
            st_before[c, h] = st.astype(BF16)
            st = st * dec[c, h] + kv[c, h]
        st_ref[h] = st
    for c, h in units:
        o = _dot(s[c, h], v_ref[rows(c), vcol(h)]) + _dot_nt(qd[rows(c), kcol(h)], st_before[c, h])
        o_ref[rows(c), vcol(h)] = (_rms_norm(o, ng) * gs_ref[rows(c), vcol(h)]).astype(o_ref.dtype)


def _gla(q, k, la, v, gs, norm_g, seq):
    t = q.shape[0]
    tc = min(TC_GLA, seq)
    nt = seq // tc
    hk = GLA_HEADS * GLA_DK
    hv = GLA_HEADS * GLA_DV
    tile_k = pl.BlockSpec((tc, hk), lambda b, n: (b * nt + n, 0))
    tile_v = pl.BlockSpec((tc, hv), lambda b, n: (b * nt + n, 0))
    vmem = 2 * tc * (3 * hk * 4 + hv * 2 + hv * 4 + hv * 2) + GLA_HEADS * GLA_DV * GLA_DK * 4 + (16 << 20)
    return pl.pallas_call(
        functools.partial(_gla_kernel, tc=tc),
        grid=(t // seq, nt),
        in_specs=[tile_k, tile_k, tile_k, tile_v, tile_v, _const_spec((1, GLA_DV))],
        out_specs=tile_v,
        out_shape=jax.ShapeDtypeStruct((t, hv), BF16),
        scratch_shapes=[pltpu.VMEM((GLA_HEADS, GLA_DV, GLA_DK), F32)],
        compiler_params=_params(("arbitrary", "arbitrary"), vmem),
        name="gla",
    )(q, k, la, v, gs, norm_g.reshape(1, GLA_DV))


def _odd_in_kernel(x_ref, mod_ref, pos_ref, freq_ref, sgn_ref, wcq, wckv, wkp, wkps, qg, kvg,
                   wqn, wqp, wqps, wkn, wvt, qn_o, qp_o, kn_o, kp_o, vt_o, *, row, scale):
    u = _modulated(x_ref[...], mod_ref, row).astype(BF16)
    cqn = _rms_norm(_dot(u, wcq[...]), qg[...]).astype(BF16)
    ckvn = _rms_norm(_dot(u, wckv[...]), kvg[...]).astype(BF16)
    ang = pos_ref[...].astype(F32) * freq_ref[...]
    cos = jnp.cos(ang)
    sin = jnp.sin(ang) * sgn_ref[...]
    kp_o[...] = (_dot(u, wkp[...]) * cos + _dot(u, wkps[...]) * sin).astype(BF16)
    cos_h = jnp.concatenate([cos] * MLA_HEADS, axis=1)
    sin_h = jnp.concatenate([sin] * MLA_HEADS, axis=1)
    qn_o[...] = (_dot(cqn, wqn[...]) * scale).astype(BF16)
    qp_o[...] = ((_dot(cqn, wqp[...]) * cos_h + _dot(cqn, wqps[...]) * sin_h) * scale).astype(BF16)
    kn_o[...] = _dot(ckvn, wkn[...]).astype(BF16)
    vt_o[...] = _dot_nt(wvt[...], ckvn).astype(BF16)


def _swap_halves(w):
    half = w.shape[-1] // 2
    return jnp.concatenate([w[..., half:], w[..., :half]], axis=-1)


def _odd_in(x, mod_l, row, positions, w_in, q_norm_g, w_q_up, kv_norm_g, w_kv_up, seq):
    t, d = x.shape
    tm = min(TM_ODD_IN, seq)
    tpb = seq // tm
    hn = MLA_HEADS * MLA_NOPE
    hv = MLA_HEADS * MLA_V
    pad_r = LANES - MLA_ROPE
    wb = w_in.astype(BF16)
    wcq = wb[:, :MLA_Q_RANK]
    wckv = wb[:, MLA_Q_RANK:MLA_Q_RANK + MLA_KV_RANK]
    wkpe = wb[:, MLA_Q_RANK + MLA_KV_RANK:]
    wkp = jnp.pad(wkpe, ((0, 0), (0, pad_r)))
    wkps = jnp.pad(_swap_halves(wkpe), ((0, 0), (0, pad_r)))
    wq = w_q_up.astype(BF16).reshape(MLA_Q_RANK, MLA_HEADS, MLA_NOPE + MLA_ROPE)
    wqn = wq[:, :, :MLA_NOPE].reshape(MLA_Q_RANK, hn)
    wq_pe = wq[:, :, MLA_NOPE:]
    pad3 = ((0, 0), (0, 0), (0, pad_r))
    wqp = jnp.pad(wq_pe, pad3).reshape(MLA_Q_RANK, MLA_HEADS * LANES)
    wqps = jnp.pad(_swap_halves(wq_pe), pad3).reshape(MLA_Q_RANK, MLA_HEADS * LANES)
    wkv = w_kv_up.astype(BF16).reshape(MLA_KV_RANK, MLA_HEADS, MLA_NOPE + MLA_V)
    wkn = wkv[:, :, :MLA_NOPE].reshape(MLA_KV_RANK, hn)
    wvt = wkv[:, :, MLA_NOPE:].reshape(MLA_KV_RANK, hv).T
    half = MLA_ROPE // 2
    freqs = ROPE_THETA ** (-jnp.arange(half, dtype=F32) / half)
    zeros = jnp.zeros((pad_r,), F32)
    freq_row = jnp.concatenate([freqs, freqs, zeros]).reshape(1, LANES)
    sgn_row = jnp.concatenate([-jnp.ones((half,), F32), jnp.ones((half,), F32), zeros]).reshape(1, LANES)
    ws = [wcq, wckv, wkp, wkps, q_norm_g.reshape(1, -1), kv_norm_g.reshape(1, -1), wqn, wqp, wqps, wkn, wvt]
    widths = [hn, MLA_HEADS * LANES, hn, LANES]
    scale = math.log2(math.e) * (MLA_NOPE + MLA_ROPE) ** -0.5
    vmem = 2 * sum(w.size * 2 for w in ws) + 2 * tm * sum(widths) * 2 + 16 * tm * d * 4 + (8 << 20)
    return pl.pallas_call(
        functools.partial(_odd_in_kernel, row=row, scale=scale),
        grid=(t // tm,),
        in_specs=([pl.BlockSpec((tm, d), lambda r: (r, 0)),
                   pl.BlockSpec((None, N_MOD, d), lambda r: (r // tpb, 0, 0)),
                   pl.BlockSpec((tm, 1), lambda r: (r, 0)),
                   _const_spec((1, LANES)), _const_spec((1, LANES))]
                  + [_const_spec(w.shape) for w in ws]),
        out_specs=([pl.BlockSpec((tm, w), lambda r: (r, 0)) for w in widths]
                   + [pl.BlockSpec((hv, tm), lambda r: (0, r))]),
        out_shape=([jax.ShapeDtypeStruct((t, w), BF16) for w in widths]
                   + [jax.ShapeDtypeStruct((hv, t), BF16)]),
        compiler_params=_params(("arbitrary",), vmem),
        name="odd_in_proj",
    )(x, mod_l, positions.reshape(t, 1), freq_row, sgn_row, *ws)


def _attn_kernel(qn_ref, qp_ref, qn_next_ref, qp_next_ref, kn_ref, kp_ref, vt_ref, o_ref,
                 s_ref, mx_ref, p_ref, acc_ref, gap_ref, *, tq):
    i = pl.program_id(2)
    def transposed(qn, qp):
        return jnp.concatenate([qn, qp], axis=1).astype(F32).T.astype(BF16)

    q = transposed(qn_ref[...], qp_ref[...])
    FIRST = 2

    def put_scores(t, slot, qmat):
        k0 = pl.multiple_of(t * tq, tq)
        kt = jnp.concatenate([kn_ref[pl.ds(k0, tq), :], kp_ref[pl.ds(k0, tq), :]], axis=1)
        st = _dot(kt, qmat)
        s_ref[slot] = st
        mx_ref[slot] = jnp.broadcast_to(jnp.max(st, axis=0, keepdims=True), (SUBLANES, tq))

    def pv(t, slot):
        k0 = pl.multiple_of(t * tq, tq)
        return _dot(vt_ref[:, pl.ds(k0, tq)], p_ref[slot])

    def causal_mask(st):
        key = lax.broadcasted_iota(jnp.int32, st.shape, 0)
        qry = lax.broadcasted_iota(jnp.int32, st.shape, 1)
        return jnp.where(key <= qry, st, -jnp.inf)

    def prefetch_next_block():
        put_scores(0, FIRST, transposed(qn_next_ref[...], qp_next_ref[...]))

    def first_tile(last):
        st = s_ref[FIRST]
        if last:
            st = causal_mask(st)
            m0 = jnp.max(st, axis=0, keepdims=True)
        else:
            m0 = mx_ref[FIRST][0:1, :]
        p = jnp.exp2(st - m0)
        l0 = jnp.sum(p, axis=0, keepdims=True)
        p_ref[0] = p.astype(BF16)
        if last:
            prefetch_next_block()
            o_ref[...] = (pv(0, 0) / l0).T.astype(o_ref.dtype)
            gap_ref[...] = jnp.zeros(gap_ref.shape, F32)
            return None
        acc_ref[...] = jnp.zeros(acc_ref.shape, F32)
        return m0, jnp.ones((1, tq), F32), l0, jnp.zeros((1, tq), F32)

    def fast_step(t, p_in, p_out, carry, last):
        m_prev, a_prev, l_prev, gap = carry
        k0 = pl.multiple_of(t * tq, tq)
        kt = jnp.concatenate([kn_ref[pl.ds(k0, tq), :], kp_ref[pl.ds(k0, tq), :]], axis=1)
        st = _dot(kt, q)
        if last:
            st = causal_mask(st)
        cmax = jnp.max(st, axis=0, keepdims=True)
        p = jnp.exp2(st - m_prev)
        lsum = jnp.sum(p, axis=0, keepdims=True)
        p_ref[p_out] = p.astype(BF16)
        gap = jnp.maximum(gap, cmax - m_prev)
        acc = (acc_ref[...] + pv(t - 1, p_in)) * a_prev
        if last:
            prefetch_next_block()
            acc = acc + pv(t, p_out)
            o_ref[...] = (acc / (l_prev + lsum)).T.astype(o_ref.dtype)
            gap_ref[...] = jnp.broadcast_to(gap, gap_ref.shape)
            return None
        acc_ref[...] = acc
        m_new = jnp.maximum(m_prev, cmax)
        alpha = jnp.exp2(m_prev - m_new)
        return m_new, alpha, (l_prev + lsum) * alpha, gap

    @pl.when(i == 0)
    def _():
        put_scores(0, FIRST, q)
        first_tile(True)

    @pl.when(i > 0)
    def _():
        carry = first_tile(False)

        def pair(n, carry):
            carry = fast_step(2 * n + 1, 0, 1, carry, False)
            return fast_step(2 * n + 2, 1, 0, carry, False)

        rest = i - 1
        carry = lax.fori_loop(0, lax.shift_right_logical(rest, 1), pair, carry)
        odd = (rest & 1) == 1

        @pl.when(odd)
        def _():
            fast_step(i, 1, 0, fast_step(i - 1, 0, 1, carry, False), True)

        @pl.when(jnp.logical_not(odd))
        def _():
            fast_step(i, 0, 1, carry, True)

    def robust_step(t, slot, carry, last):
        m_prev, l_prev = carry
        st = s_ref[slot]
        if last:
            st = causal_mask(st)
            m_cur = jnp.max(st, axis=0, keepdims=True)
        else:
            m_cur = mx_ref[slot][0:1, :]
        m_new = jnp.maximum(m_prev, m_cur)
        alpha = jnp.exp2(m_prev - m_new)
        p = jnp.exp2(st - m_new)
        l_new = alpha * l_prev + jnp.sum(p, axis=0, keepdims=True)
        p_ref[slot] = p.astype(BF16)
        if not last:
            put_scores(t + 1, 1 - slot, q)
        acc = alpha * (acc_ref[...] + pv(jnp.maximum(t - 1, 0), 1 - slot))
        if last:
            acc = acc + pv(t, slot)
            o_ref[...] = (acc / l_new).T.astype(o_ref.dtype)
        else:
            acc_ref[...] = acc
        return m_new, l_new

    @pl.when(jnp.max(gap_ref[0:1, :]) > MAX_SAFE_GAP)
    def _():
        put_scores(0, 0, q)
        p_ref[1] = jnp.zeros(p_ref.shape[1:], BF16)
        acc_ref[...] = jnp.zeros(acc_ref.shape, F32)
        carry = (jnp.full((1, tq), -jnp.inf, F32), jnp.zeros((1, tq), F32))

        def pair(n, carry):
            return robust_step(2 * n + 1, 1, robust_step(2 * n, 0, carry, False), False)

        carry = lax.fori_loop(0, lax.shift_right_logical(i, 1), pair, carry)
        odd = (i & 1) == 1

        @pl.when(odd)
        def _():
            robust_step(i, 1, robust_step(i - 1, 0, carry, False), True)

        @pl.when(jnp.logical_not(odd))
        def _():
            robust_step(i, 0, carry, True)


def _attention(qn, qp, kn, kp, vt, seq):
    t = qn.shape[0]
    bsz = t // seq
    tq = min(TQ_ATTN, seq)
    nq = seq // tq
    q_spec = pl.BlockSpec((tq, LANES), lambda b, h, i: (b * nq + i, h))
    kn_spec = pl.BlockSpec((seq, LANES), lambda b, h, i: (b, h))
    kp_spec = pl.BlockSpec((seq, LANES), lambda b, h, i: (b, 0))
    vt_spec = pl.BlockSpec((LANES, seq), lambda b, h, i: (h, b))
    q_next_spec = pl.BlockSpec((tq, LANES), lambda b, h, i: (b * nq + jnp.minimum(i + 1, nq - 1), h))
    vmem = 2 * 3 * seq * LANES * 2 + 10 * tq * LANES * 2 + 3 * tq * tq * 4 + 2 * tq * tq * 2 + 8 * tq * LANES * 4 + (8 << 20)
    return pl.pallas_call(
        functools.partial(_attn_kernel, tq=tq),
        grid=(bsz, MLA_HEADS, nq),
        in_specs=[q_spec, q_spec, q_next_spec, q_next_spec, kn_spec, kp_spec, vt_spec],
        out_specs=q_spec,
        out_shape=jax.ShapeDtypeStruct((t, MLA_HEADS * MLA_V), BF16),
        scratch_shapes=[pltpu.VMEM((3, tq, tq), F32), pltpu.VMEM((3, SUBLANES, tq), F32),
                        pltpu.VMEM((2, tq, tq), BF16), pltpu.VMEM((MLA_V, tq), F32),
                        pltpu.VMEM((SUBLANES, tq), F32)],
        compiler_params=_params(("arbitrary", "arbitrary", "arbitrary"), vmem),
        name="mla_attention",
    )(qn, qp, qn, qp, kn, kp, vt)


def kernel(x, c, positions, mod_w, mod_b, ffn_in, ffn_out, ln_g, ln_b, even_w_in, conv_w, conv_b, rg_wa, rg_ba, rg_wx, rg_bx, rg_lam, gla_wa2, gla_ba, gla_norm_g, even_w_out, odd_w_in, q_norm_g, w_q_up, kv_norm_g, w_kv_up, odd_w_out):
    bsz, seq, d = x.shape
    depth = mod_w.shape[0]
    alpha = (2.0 * depth) ** 0.25
    mod = _mod_vectors(c, mod_w, mod_b)
    h = x.reshape(bsz * seq, d)
    for l in range(depth):
        mod_l = mod[l]
        h = _ffn_sublayer(h, mod_l, 0, ffn_in[l, 0], ffn_out[l, 0], ln_g[l, 0], ln_b[l, 0], alpha, seq)
        if l % 2 == 0:
            e = l // 2
            xa, gg, q, k, v, gs, la = _even_in(h, mod_l, 3, even_w_in[e], gla_wa2[e], gla_ba[e], seq)
            ya = _rglru(xa, gg, conv_w[e], conv_b[e], rg_wa[e], rg_ba[e], rg_wx[e], rg_bx[e], rg_lam[e], seq)
            yb = _gla(q, k, la, v, gs, gla_norm_g[e], seq)
            wo = even_w_out[e].astype(BF16)
            rgw = ya.shape[1]
            h = _proj_deepnorm(h, mod_l, 3, [ya, yb], [wo[:rgw], wo[rgw:]], ln_g[l, 1], ln_b[l, 1], alpha, seq)
        else:
            o = l // 2
            qn, qp, kn, kp, v = _odd_in(h, mod_l, 3, positions, odd_w_in[o], q_norm_g[o], w_q_up[o],
                                        kv_norm_g[o], w_kv_up[o], seq)
            att = _attention(qn, qp, kn, kp, v, seq)
            h = _proj_deepnorm(h, mod_l, 3, [att], [odd_w_out[o].astype(BF16)], ln_g[l, 1], ln_b[l, 1], alpha, seq)
        h = _ffn_sublayer(h, mod_l, 6, ffn_in[l, 1], ffn_out[l, 1], ln_g[l, 2], ln_b[l, 2], alpha, seq)
    return h.reshape(bsz, seq, d)
```

```python
import functools
import math

import jax
import jax.numpy as jnp
from jax import lax
from jax.experimental import pallas as pl
from jax.experimental.pallas import tpu as pltpu

F32 = jnp.float32
BF16 = jnp.bfloat16

LN_EPS = 1e-5
RG_BLOCKS = 16
RG_C = 8.0
CONV_W = 4
GLA_HEADS = 4
GLA_DK = 128
GLA_DV = 256
GLA_RANK = 16
GLA_TAU = 16.0
GLA_CHUNK = 64
MLA_HEADS = 8
MLA_NOPE = 128
MLA_ROPE = 64
MLA_V = 128
MLA_Q_RANK = 256
MLA_KV_RANK = 128
ROPE_THETA = 10000.0
N_MOD = 9

LANES = 128
SUBLANES = 8
VMEM_CAP = 56 << 20
MAX_SAFE_GAP = 96.0

TM_FFN = 1024
TM_PROJ = 512
TM_EVEN_IN = 512
TM_ODD_IN = 512
TC_RGLRU = 512
TC_GLA = 256
TQ_ATTN = 1024
FF_CHUNK = 256


def _params(semantics, vmem_bytes):
    return pltpu.CompilerParams(dimension_semantics=semantics,
                                vmem_limit_bytes=int(min(max(vmem_bytes, 16 << 20), VMEM_CAP)))


def _const_spec(shape):
    nd = len(shape)
    return pl.BlockSpec(shape, lambda *_: (0,) * nd, pipeline_mode=pl.Buffered(1))


def _dot(a, b):
    return jnp.dot(a, b, preferred_element_type=F32)


def _dot_nt(a, b):
    return lax.dot_general(a, b, (((1,), (1,)), ((), ())), preferred_element_type=F32)


def _dot_tn(a, b):
    return lax.dot_general(a, b, (((0,), (0,)), ((), ())), preferred_element_type=F32)


def _sigmoid(x):
    return 1.0 / (1.0 + jnp.exp(-x))


def _silu(x):
    return x * _sigmoid(x)


def _softplus(x):
    return jnp.maximum(x, 0.0) + jnp.log1p(jnp.exp(-jnp.abs(x)))


def _log_sigmoid(x):
    return jnp.minimum(x, 0.0) - jnp.log1p(jnp.exp(-jnp.abs(x)))


def _gelu_tanh(x):
    c = math.sqrt(2.0 / math.pi)
    return x * (0.5 * (1.0 + jnp.tanh(c * (x + 0.044715 * (x * x * x)))))


def _layer_norm(y, g, b):
    mu = jnp.mean(y, axis=-1, keepdims=True)
    d = y - mu
    var = jnp.mean(d * d, axis=-1, keepdims=True)
    return d * lax.rsqrt(var + LN_EPS) * g + b


def _rms_norm(y, g):
    return y * lax.rsqrt(jnp.mean(y * y, axis=-1, keepdims=True) + LN_EPS) * g


def _modulated(x, mod_ref, row):
    return x * (1.0 + mod_ref[row + 1:row + 2, :]) + mod_ref[row:row + 1, :]


def _mod_kernel(c_ref, w_ref, b_ref, o_ref):
    ca = _silu(c_ref[...]).astype(BF16)
    o_ref[...] = _dot(ca, w_ref[...].astype(BF16)) + b_ref[...]


def _mod_vectors(c, mod_w, mod_b):
    depth, d, n = mod_w.shape
    bsz = c.shape[0]
    tn = d
    out = pl.pallas_call(
        _mod_kernel,
        grid=(depth, n // tn),
        in_specs=[pl.BlockSpec((bsz, d), lambda l, j: (0, 0)),
                  pl.BlockSpec((None, d, tn), lambda l, j: (l, 0, j)),
                  pl.BlockSpec((None, 1, tn), lambda l, j: (l, 0, j))],
        out_specs=pl.BlockSpec((None, bsz, tn), lambda l, j: (l, 0, j)),
        out_shape=jax.ShapeDtypeStruct((depth, bsz, n), F32),
        compiler_params=_params(("arbitrary", "arbitrary"), 4 * d * tn * 4),
        name="adaln_mod",
    )(c, mod_w, mod_b.reshape(depth, 1, n))
    return out.reshape(depth, bsz, N_MOD, d)


def _ffn_kernel(x_ref, mod_ref, wg_ref, wu_ref, wo_ref, lng_ref, lnb_ref, o_ref, act_ref,
                *, row, alpha, chunk):
    x = x_ref[...]
    u = _modulated(x, mod_ref, row).astype(BF16)
    d_ff = wg_ref.shape[1]
    for j in range(d_ff // chunk):
        sl = slice(j * chunk, (j + 1) * chunk)
        g = _dot(u, wg_ref[:, sl])
        p = _dot(u, wu_ref[:, sl])
        act_ref[:, sl] = (_silu(g) * p).astype(BF16)
    gate = mod_ref[row + 2:row + 3, :]
    tm = x_ref.shape[0]
    halves = [slice(0, tm // 2), slice(tm // 2, tm)]
    fs = [_dot(act_ref[rs, :], wo_ref[...]) for rs in halves]
    for rs, f in zip(halves, fs):
        y = alpha * x_ref[rs, :] + (0.5 * (1.0 + gate)) * f
        o_ref[rs, :] = _layer_norm(y, lng_ref[...], lnb_ref[...])


def _ffn_sublayer(x, mod_l, row, w_in, w_out, ln_g, ln_b, alpha, seq):
    t, d = x.shape
    d_ff = w_out.shape[0]
    tm = min(TM_FFN, seq)
    tpb = seq // tm
    wg = w_in[:, :d_ff].astype(BF16)
    wu = w_in[:, d_ff:].astype(BF16)
    wo = w_out.astype(BF16)
    vmem = 3 * d * d_ff * 2 + 4 * tm * d * 4 + tm * d_ff * 2 + 6 * tm * d * 4 + (8 << 20)
    return pl.pallas_call(
        functools.partial(_ffn_kernel, row=row, alpha=alpha, chunk=min(FF_CHUNK, d_ff)),
        grid=(t // tm,),
        in_specs=[pl.BlockSpec((tm, d), lambda r: (r, 0)),
                  pl.BlockSpec((None, N_MOD, d), lambda r: (r // tpb, 0, 0)),
                  _const_spec((d, d_ff)), _const_spec((d, d_ff)), _const_spec((d_ff, d)),
                  _const_spec((1, d)), _const_spec((1, d))],
        out_specs=pl.BlockSpec((tm, d), lambda r: (r, 0)),
        out_shape=jax.ShapeDtypeStruct((t, d), F32),
        scratch_shapes=[pltpu.VMEM((tm, d_ff), BF16)],
        compiler_params=_params(("arbitrary",), vmem),
        name="ffn_sublayer",
    )(x, mod_l, wg, wu, wo, ln_g.reshape(1, d), ln_b.reshape(1, d))


def _proj_dn_kernel(*refs, n_in, row, alpha):
    x_ref, mod_ref = refs[0], refs[1]
    a_refs = refs[2:2 + n_in]
    w_refs = refs[2 + n_in:2 + 2 * n_in]
    lng_ref, lnb_ref, o_ref = refs[2 + 2 * n_in:]
    gate = mod_ref[row + 2:row + 3, :]
    tm = x_ref.shape[0]
    halves = [slice(0, tm // 2), slice(tm // 2, tm)]
    ms = []
    for rs in halves:
        m = _dot(a_refs[0][rs, :], w_refs[0][...])
        for a_ref, w_ref in zip(a_refs[1:], w_refs[1:]):
            m = m + _dot(a_ref[rs, :], w_ref[...])
        ms.append(m)
    for rs, m in zip(halves, ms):
        y = alpha * x_ref[rs, :] + (1.0 + gate) * m
        o_ref[rs, :] = _layer_norm(y, lng_ref[...], lnb_ref[...])


def _proj_deepnorm(x, mod_l, row, acts, weights, ln_g, ln_b, alpha, seq):
    t, d = x.shape
    tm = min(TM_PROJ, seq)
    tpb = seq // tm
    n_in = len(acts)
    row_spec = lambda w: pl.BlockSpec((tm, w), lambda r: (r, 0))
    vmem = sum(2 * w.size * 2 for w in weights) + sum(2 * tm * a.shape[1] * 2 for a in acts) + 8 * tm * d * 4 + (4 << 20)
    return pl.pallas_call(
        functools.partial(_proj_dn_kernel, n_in=n_in, row=row, alpha=alpha),
        grid=(t // tm,),
        in_specs=([row_spec(d), pl.BlockSpec((None, N_MOD, d), lambda r: (r // tpb, 0, 0))]
                  + [row_spec(a.shape[1]) for a in acts]
                  + [_const_spec(w.shape) for w in weights]
                  + [_const_spec((1, d)), _const_spec((1, d))]),
        out_specs=row_spec(d),
        out_shape=jax.ShapeDtypeStruct((t, d), F32),
        compiler_params=_params(("arbitrary",), vmem),
        name="proj_deepnorm",
    )(x, mod_l, *acts, *weights, ln_g.reshape(1, d), ln_b.reshape(1, d))


def _even_in_kernel(x_ref, mod_ref, wxa, wga, wq, wkz, wv, wg, wa2, ba,
                    xa_o, gg_o, q_o, k_o, v_o, gs_o, la_o, *, row):
    u = _modulated(x_ref[...], mod_ref, row).astype(BF16)
    xa_o[...] = _dot(u, wxa[...])
    gg_o[...] = _gelu_tanh(_dot(u, wga[...]))
    q_o[...] = _dot(u, wq[...]) * (GLA_DK ** -0.5)
    kz = _dot(u, wkz[...])
    hk = k_o.shape[1]
    k_o[...] = kz[:, :hk]
    v_o[...] = _dot(u, wv[...]).astype(BF16)
    gs_o[...] = _silu(_dot(u, wg[...]))
    za = kz[:, hk:].astype(BF16)
    z = _dot(za, wa2[...]) + ba[...]
    la_o[...] = _log_sigmoid(z) * (1.0 / GLA_TAU)


def _even_in(x, mod_l, row, w_in, gla_wa2, gla_ba, seq):
    t, d = x.shape
    tm = min(TM_EVEN_IN, seq)
    tpb = seq // tm
    hk = GLA_HEADS * GLA_DK
    hv = GLA_HEADS * GLA_DV
    offs = [0, d, 2 * d, 2 * d + hk, 2 * d + 2 * hk, 2 * d + 2 * hk + hv, 2 * d + 2 * hk + 2 * hv]
    wb = w_in.astype(BF16)
    wxa, wga, wq, wk, wv, wg = [wb[:, offs[i]:offs[i + 1]] for i in range(6)]
    wza = jnp.pad(wb[:, offs[6]:], ((0, 0), (0, LANES - GLA_RANK)))
    wa2 = jnp.pad(gla_wa2.astype(BF16), ((0, LANES - GLA_RANK), (0, 0)))
    ws = [wxa, wga, wq, jnp.concatenate([wk, wza], axis=1), wv, wg, wa2]
    widths = [d, d, hk, hk, hv, hv, hk]
    dtypes = [F32, F32, F32, F32, BF16, F32, F32]
    vmem = sum(w.size * 2 for w in ws) + 2 * tm * sum(widths) * 4 + 4 * tm * d * 4 + (8 << 20)
    return pl.pallas_call(
        functools.partial(_even_in_kernel, row=row),
        grid=(t // tm,),
        in_specs=([pl.BlockSpec((tm, d), lambda r: (r, 0)),
                   pl.BlockSpec((None, N_MOD, d), lambda r: (r // tpb, 0, 0))]
                  + [_const_spec(w.shape) for w in ws] + [_const_spec((1, hk))]),
        out_specs=[pl.BlockSpec((tm, w), lambda r: (r, 0)) for w in widths],
        out_shape=[jax.ShapeDtypeStruct((t, w), dt) for w, dt in zip(widths, dtypes)],
        compiler_params=_params(("arbitrary",), vmem),
        name="even_in_proj",
    )(x, mod_l, *ws, gla_ba.reshape(1, hk))


def _rglru_kernel(xa_ref, gg_ref, cw_ref, cb_ref, wa_ref, wx_ref, ba_ref, bx_ref, lam_ref,
                  o_ref, xpad_ref, a_ref, u_ref, carry_ref, *, tc, groups):
    first = pl.program_id(1) == 0

    @pl.when(first)
    def _():
        xpad_ref[0:SUBLANES, :] = jnp.zeros((SUBLANES, xpad_ref.shape[1]), F32)
        carry_ref[...] = jnp.zeros(carry_ref.shape, F32)

    x = xa_ref[...]
    xpad_ref[SUBLANES:, :] = x
    xc = cw_ref[CONV_W - 1:CONV_W, :] * x + cb_ref[...]
    for dlt in range(1, CONV_W):
        xc = xc + cw_ref[CONV_W - 1 - dlt:CONV_W - dlt, :] * xpad_ref[pl.ds(SUBLANES - dlt, tc), :]
    xpad_ref[0:SUBLANES, :] = x[tc - SUBLANES:, :]

    xb = xc.astype(BF16)
    gw = xb.shape[1] // groups
    neg_c_sp = -RG_C * _softplus(-lam_ref[...])
    for g in range(groups):
        sl = slice(g * gw, (g + 1) * gw)
        r = _sigmoid(_dot(xb[:, sl], wa_ref[g]) + ba_ref[:, sl])
        i = _sigmoid(_dot(xb[:, sl], wx_ref[g]) + bx_ref[:, sl])
        log_a = neg_c_sp[:, sl] * r
        a = jnp.exp(log_a)
        a_ref[:, sl] = a
        z = -jnp.tanh(log_a) * (a * a + 1.0)
        root = jnp.where(z > 0.0, z * lax.rsqrt(z), 0.0)
        u_ref[:, sl] = root * (i * xc[:, sl])

    width = a_ref.shape[1]
    rows = lax.broadcasted_iota(jnp.int32, (SUBLANES, width), 0)

    def body(n, carry):
        r0 = pl.multiple_of(n * SUBLANES, SUBLANES)
        a = a_ref[pl.ds(r0, SUBLANES), :]
        b = u_ref[pl.ds(r0, SUBLANES), :]
        for dlt in (1, 2, 4):
            keep = rows >= dlt
            a_sh = jnp.where(keep, pltpu.roll(a, dlt, 0), 1.0)
            b_sh = jnp.where(keep, pltpu.roll(b, dlt, 0), 0.0)
            b = a * b_sh + b
            a = a * a_sh
        h = b + a * carry
        o_ref[pl.ds(r0, SUBLANES), :] = (h * gg_ref[pl.ds(r0, SUBLANES), :]).astype(o_ref.dtype)
        return jnp.broadcast_to(h[SUBLANES - 1:SUBLANES, :], h.shape)

    carry_ref[...] = lax.fori_loop(0, tc // SUBLANES, body, carry_ref[...], unroll=2)


def _rglru(xa, gg, conv_w, conv_b, rg_wa, rg_ba, rg_wx, rg_bx, rg_lam, seq):
    t, c = xa.shape
    tc = min(TC_RGLRU, seq)
    nt = seq // tc
    bs = c // RG_BLOCKS
    per = (2 * LANES) // bs
    groups = RG_BLOCKS // per

    def block_diag(w):
        w4 = w.reshape(groups, per, bs, bs)
        eye = jnp.eye(per, dtype=w.dtype)
        return jnp.einsum("gaij,ac->gaicj", w4, eye).reshape(groups, per * bs, per * bs).astype(BF16)

    vec = lambda v: v.reshape(1, c)
    tile = pl.BlockSpec((tc, c), lambda b, n: (b * nt + n, 0))
    vmem = 4 * tc * c * 4 + 2 * tc * c * 2 + 3 * tc * c * 4 + 8 * tc * c * 4 + (4 << 20)
    return pl.pallas_call(
        functools.partial(_rglru_kernel, tc=tc, groups=groups),
        grid=(t // seq, nt),
        in_specs=[tile, tile, _const_spec((CONV_W, c)), _const_spec((1, c)),
                  _const_spec((groups, per * bs, per * bs)), _const_spec((groups, per * bs, per * bs)),
                  _const_spec((1, c)), _const_spec((1, c)), _const_spec((1, c))],
        out_specs=tile,
        out_shape=jax.ShapeDtypeStruct((t, c), BF16),
        scratch_shapes=[pltpu.VMEM((tc + SUBLANES, c), F32), pltpu.VMEM((tc, c), F32),
                        pltpu.VMEM((tc, c), F32), pltpu.VMEM((SUBLANES, c), F32)],
        compiler_params=_params(("arbitrary", "arbitrary"), vmem),
        name="rglru",
    )(xa, gg, conv_w, vec(conv_b), block_diag(rg_wa), block_diag(rg_wx), vec(rg_ba), vec(rg_bx), vec(rg_lam))


def _gla_kernel(q_ref, k_ref, la_ref, v_ref, gs_ref, ng_ref, o_ref, st_ref, *, tc):
    @pl.when(pl.program_id(1) == 0)
    def _():
        st_ref[...] = jnp.zeros(st_ref.shape, F32)

    ch = GLA_CHUNK
    nc = tc // ch
    hk = GLA_HEADS * GLA_DK
    ri = lax.broadcasted_iota(jnp.int32, (tc, tc), 0)
    ci = lax.broadcasted_iota(jnp.int32, (tc, tc), 1)
    sh = ch.bit_length() - 1
    same_chunk = jnp.right_shift(ri, sh) == jnp.right_shift(ci, sh)
    tri = jnp.where(same_chunk, jnp.where(ri >= ci, 1.0, 0.0), 0.0).astype(BF16)
    causal = lax.broadcasted_iota(jnp.int32, (ch, ch), 0) >= lax.broadcasted_iota(jnp.int32, (ch, ch), 1)
    ng = ng_ref[...]
    units = [(c, h) for c in range(nc) for h in range(GLA_HEADS)]
    rows = lambda c: slice(c * ch, (c + 1) * ch)
    kcol = lambda h: slice(h * GLA_DK, (h + 1) * GLA_DK)
    vcol = lambda h: slice(h * GLA_DV, (h + 1) * GLA_DV)

    la = la_ref[...]
    hi = la.astype(BF16)
    r1 = la - hi.astype(F32)
    mid = r1.astype(BF16)
    lo = (r1 - mid.astype(F32)).astype(BF16)
    bc3 = _dot(tri, jnp.concatenate([hi, mid, lo], axis=1))
    bc = bc3[:, :hk] + bc3[:, hk:2 * hk] + bc3[:, 2 * hk:]
    kk = k_ref[...]
    qd = (q_ref[...] * jnp.exp(bc)).astype(BF16)
    ki = (kk * jnp.exp(-bc)).astype(BF16)
    b_last = [bc[(c + 1) * ch - 1:(c + 1) * ch, :] for c in range(nc)]
    ke = [(kk[rows(c), :] * jnp.exp(b_last[c] - bc[rows(c), :])).astype(BF16) for c in range(nc)]
    dec = {(c, h): jnp.exp(bc[(c + 1) * ch - 1:(c + 1) * ch, kcol(h)]) for c, h in units}

    s = {(c, h): jnp.where(causal, _dot_nt(qd[rows(c), kcol(h)], ki[rows(c), kcol(h)]), 0.0).astype(BF16)
         for c, h in units}
    kv = {(c, h): _dot_tn(v_ref[rows(c), vcol(h)], ke[c][:, kcol(h)]) for c, h in units}
    st_before = {}
    for h in range(GLA_HEADS):
        st = st_ref[h]
        for c in range(nc):
            st_before[c, h] = st.astype(BF16)
            st = st * dec[c, h] + kv[c, h]
        st_ref[h] = st
    for c, h in units:
        o = _dot(s[c, h], v_ref[rows(c), vcol(h)]) + _dot_nt(qd[rows(c), kcol(h)], st_before[c, h])
        o_ref[rows(c), vcol(h)] = (_rms_norm(o, ng) * gs_ref[rows(c), vcol(h)]).astype(o_ref.dtype)


def _gla(q, k, la, v, gs, norm_g, seq):
    t = q.shape[0]
    tc = min(TC_GLA, seq)
    nt = seq // tc
    hk = GLA_HEADS * GLA_DK
    hv = GLA_HEADS * GLA_DV
    tile_k = pl.BlockSpec((tc, hk), lambda b, n: (b * nt + n, 0))
    tile_v = pl.BlockSpec((tc, hv), lambda b, n: (b * nt + n, 0))
    vmem = 2 * tc * (3 * hk * 4 + hv * 2 + hv * 4 + hv * 2) + GLA_HEADS * GLA_DV * GLA_DK * 4 + (16 << 20)
    return pl.pallas_call(
        functools.partial(_gla_kernel, tc=tc),
        grid=(t // seq, nt),
        in_specs=[tile_k, tile_k, tile_k, tile_v, tile_v, _const_spec((1, GLA_DV))],
        out_specs=tile_v,
        out_shape=jax.ShapeDtypeStruct((t, hv), BF16),
        scratch_shapes=[pltpu.VMEM((GLA_HEADS, GLA_DV, GLA_DK), F32)],
        compiler_params=_params(("arbitrary", "arbitrary"), vmem),
        name="gla",
    )(q, k, la, v, gs, norm_g.reshape(1, GLA_DV))


def _odd_in_kernel(x_ref, mod_ref, pos_ref, freq_ref, sgn_ref, wcq, wckv, wkp, wkps, qg, kvg,
                   wqn, wqp, wqps, wkn, wvt, qn_o, qp_o, kn_o, kp_o, vt_o, *, row, scale):
    u = _modulated(x_ref[...], mod_ref, row).astype(BF16)
    cqn = _rms_norm(_dot(u, wcq[...]), qg[...]).astype(BF16)
    ckvn = _rms_norm(_dot(u, wckv[...]), kvg[...]).astype(BF16)
    ang = pos_ref[...].astype(F32) * freq_ref[...]
    cos = jnp.cos(ang)
    sin = jnp.sin(ang) * sgn_ref[...]
    kp_o[...] = (_dot(u, wkp[...]) * cos + _dot(u, wkps[...]) * sin).astype(BF16)
    cos_h = jnp.concatenate([cos] * MLA_HEADS, axis=1)
    sin_h = jnp.concatenate([sin] * MLA_HEADS, axis=1)
    qn_o[...] = (_dot(cqn, wqn[...]) * scale).astype(BF16)
    qp_o[...] = ((_dot(cqn, wqp[...]) * cos_h + _dot(cqn, wqps[...]) * sin_h) * scale).astype(BF16)
    kn_o[...] = _dot(ckvn, wkn[...]).astype(BF16)
    vt_o[...] = _dot_nt(wvt[...], ckvn).astype(BF16)


def _swap_halves(w):
    half = w.shape[-1] // 2
    return jnp.concatenate([w[..., half:], w[..., :half]], axis=-1)


def _odd_in(x, mod_l, row, positions, w_in, q_norm_g, w_q_up, kv_norm_g, w_kv_up, seq):
    t, d = x.shape
    tm = min(TM_ODD_IN, seq)
    tpb = seq // tm
    hn = MLA_HEADS * MLA_NOPE
    hv = MLA_HEADS * MLA_V
    pad_r = LANES - MLA_ROPE
    wb = w_in.astype(BF16)
    wcq = wb[:, :MLA_Q_RANK]
    wckv = wb[:, MLA_Q_RANK:MLA_Q_RANK + MLA_KV_RANK]
    wkpe = wb[:, MLA_Q_RANK + MLA_KV_RANK:]
    wkp = jnp.pad(wkpe, ((0, 0), (0, pad_r)))
    wkps = jnp.pad(_swap_halves(wkpe), ((0, 0), (0, pad_r)))
    wq = w_q_up.astype(BF16).reshape(MLA_Q_RANK, MLA_HEADS, MLA_NOPE + MLA_ROPE)
    wqn = wq[:, :, :MLA_NOPE].reshape(MLA_Q_RANK, hn)
    wq_pe = wq[:, :, MLA_NOPE:]
    pad3 = ((0, 0), (0, 0), (0, pad_r))
    wqp = jnp.pad(wq_pe, pad3).reshape(MLA_Q_RANK, MLA_HEADS * LANES)
    wqps = jnp.pad(_swap_halves(wq_pe), pad3).reshape(MLA_Q_RANK, MLA_HEADS * LANES)
    wkv = w_kv_up.astype(BF16).reshape(MLA_KV_RANK, MLA_HEADS, MLA_NOPE + MLA_V)
    wkn = wkv[:, :, :MLA_NOPE].reshape(MLA_KV_RANK, hn)
    wvt = wkv[:, :, MLA_NOPE:].reshape(MLA_KV_RANK, hv).T
    half = MLA_ROPE // 2
    freqs = ROPE_THETA ** (-jnp.arange(half, dtype=F32) / half)
    zeros = jnp.zeros((pad_r,), F32)
    freq_row = jnp.concatenate([freqs, freqs, zeros]).reshape(1, LANES)
    sgn_row = jnp.concatenate([-jnp.ones((half,), F32), jnp.ones((half,), F32), zeros]).reshape(1, LANES)
    ws = [wcq, wckv, wkp, wkps, q_norm_g.reshape(1, -1), kv_norm_g.reshape(1, -1), wqn, wqp, wqps, wkn, wvt]
    widths = [hn, MLA_HEADS * LANES, hn, LANES]
    scale = math.log2(math.e) * (MLA_NOPE + MLA_ROPE) ** -0.5
    vmem = 2 * sum(w.size * 2 for w in ws) + 2 * tm * sum(widths) * 2 + 16 * tm * d * 4 + (8 << 20)
    return pl.pallas_call(
        functools.partial(_odd_in_kernel, row=row, scale=scale),
        grid=(t // tm,),
        in_specs=([pl.BlockSpec((tm, d), lambda r: (r, 0)),
                   pl.BlockSpec((None, N_MOD, d), lambda r: (r // tpb, 0, 0)),
                   pl.BlockSpec((tm, 1), lambda r: (r, 0)),
                   _const_spec((1, LANES)), _const_spec((1, LANES))]
                  + [_const_spec(w.shape) for w in ws]),
        out_specs=([pl.BlockSpec((tm, w), lambda r: (r, 0)) for w in widths]
                   + [pl.BlockSpec((hv, tm), lambda r: (0, r))]),
        out_shape=([jax.ShapeDtypeStruct((t, w), BF16) for w in widths]
                   + [jax.ShapeDtypeStruct((hv, t), BF16)]),
        compiler_params=_params(("arbitrary",), vmem),
        name="odd_in_proj",
    )(x, mod_l, positions.reshape(t, 1), freq_row, sgn_row, *ws)


def _attn_kernel(qn_ref, qp_ref, qn_next_ref, qp_next_ref, kn_ref, kp_ref, vt_ref, o_ref,
                 s_ref, mx_ref, p_ref, acc_ref, gap_ref, *, tq):
    i = pl.program_id(2)
    def transposed(qn, qp):
        return jnp.concatenate([qn, qp], axis=1).astype(F32).T.astype(BF16)

    q = transposed(qn_ref[...], qp_ref[...])
    FIRST = 2

    def put_scores(t, slot, qmat):
        k0 = pl.multiple_of(t * tq, tq)
        kt = jnp.concatenate([kn_ref[pl.ds(k0, tq), :], kp_ref[pl.ds(k0, tq), :]], axis=1)
        st = _dot(kt, qmat)
        s_ref[slot] = st
        mx_ref[slot] = jnp.broadcast_to(jnp.max(st, axis=0, keepdims=True), (SUBLANES, tq))

    def pv(t, slot):
        k0 = pl.multiple_of(t * tq, tq)
        return _dot(vt_ref[:, pl.ds(k0, tq)], p_ref[slot])

    def causal_mask(st):
        key = lax.broadcasted_iota(jnp.int32, st.shape, 0)
        qry = lax.broadcasted_iota(jnp.int32, st.shape, 1)
        return jnp.where(key <= qry, st, -jnp.inf)

    def prefetch_next_block():
        put_scores(0, FIRST, transposed(qn_next_ref[...], qp_next_ref[...]))

    def first_tile(last):
        st = s_ref[FIRST]
        if last:
            st = causal_mask(st)
            m0 = jnp.max(st, axis=0, keepdims=True)
        else:
            m0 = mx_ref[FIRST][0:1, :]
        p = jnp.exp2(st - m0)
        l0 = jnp.sum(p, axis=0, keepdims=True)
        p_ref[0] = p.astype(BF16)
        if last:
            prefetch_next_block()
            o_ref[...] = (pv(0, 0) / l0).T.astype(o_ref.dtype)
            gap_ref[...] = jnp.zeros(gap_ref.shape, F32)
            return None
        acc_ref[...] = jnp.zeros(acc_ref.shape, F32)
        return m0, jnp.ones((1, tq), F32), l0, jnp.zeros((1, tq), F32)

    def fast_step(t, p_in, p_out, carry, last):
        m_prev, a_prev, l_prev, gap = carry
        k0 = pl.multiple_of(t * tq, tq)
        kt = jnp.concatenate([kn_ref[pl.ds(k0, tq), :], kp_ref[pl.ds(k0, tq), :]], axis=1)
        st = _dot(kt, q)
        if last:
            st = causal_mask(st)
        cmax = jnp.max(st, axis=0, keepdims=True)
        p = jnp.exp2(st - m_prev)
        lsum = jnp.sum(p, axis=0, keepdims=True)
        p_ref[p_out] = p.astype(BF16)
        gap = jnp.maximum(gap, cmax - m_prev)
        acc = (acc_ref[...] + pv(t - 1, p_in)) * a_prev
        if last:
            prefetch_next_block()
            acc = acc + pv(t, p_out)
            o_ref[...] = (acc / (l_prev + lsum)).T.astype(o_ref.dtype)
            gap_ref[...] = jnp.broadcast_to(gap, gap_ref.shape)
            return None
        acc_ref[...] = acc
        m_new = jnp.maximum(m_prev, cmax)
        alpha = jnp.exp2(m_prev - m_new)
        return m_new, alpha, (l_prev + lsum) * alpha, gap

    @pl.when(i == 0)
    def _():
        put_scores(0, FIRST, q)
        first_tile(True)

    @pl.when(i > 0)
    def _():
        carry = first_tile(False)

        def pair(n, carry):
            carry = fast_step(2 * n + 1, 0, 1, carry, False)
            return fast_step(2 * n + 2, 1, 0, carry, False)

        rest = i - 1
        carry = lax.fori_loop(0, lax.shift_right_logical(rest, 1), pair, carry)
        odd = (rest & 1) == 1

        @pl.when(odd)
        def _():
            fast_step(i, 1, 0, fast_step(i - 1, 0, 1, carry, False), True)

        @pl.when(jnp.logical_not(odd))
        def _():
            fast_step(i, 0, 1, carry, True)

    def robust_step(t, slot, carry, last):
        m_prev, l_prev = carry
        st = s_ref[slot]
        if last:
            st = causal_mask(st)
            m_cur = jnp.max(st, axis=0, keepdims=True)
        else:
            m_cur = mx_ref[slot][0:1, :]
        m_new = jnp.maximum(m_prev, m_cur)
        alpha = jnp.exp2(m_prev - m_new)
        p = jnp.exp2(st - m_new)
        l_new = alpha * l_prev + jnp.sum(p, axis=0, keepdims=True)
        p_ref[slot] = p.astype(BF16)
        if not last:
            put_scores(t + 1, 1 - slot, q)
        acc = alpha * (acc_ref[...] + pv(jnp.maximum(t - 1, 0), 1 - slot))
        if last:
            acc = acc + pv(t, slot)
            o_ref[...] = (acc / l_new).T.astype(o_ref.dtype)
        else:
            acc_ref[...] = acc
        return m_new, l_new

    @pl.when(jnp.max(gap_ref[0:1, :]) > MAX_SAFE_GAP)
    def _():
        put_scores(0, 0, q)
        p_ref[1] = jnp.zeros(p_ref.shape[1:], BF16)
        acc_ref[...] = jnp.zeros(acc_ref.shape, F32)
        carry = (jnp.full((1, tq), -jnp.inf, F32), jnp.zeros((1, tq), F32))

        def pair(n, carry):
            return robust_step(2 * n + 1, 1, robust_step(2 * n, 0, carry, False), False)

        carry = lax.fori_loop(0, lax.shift_right_logical(i, 1), pair, carry)
        odd = (i & 1) == 1

        @pl.when(odd)
        def _():
            robust_step(i, 1, robust_step(i - 1, 0, carry, False), True)

        @pl.when(jnp.logical_not(odd))
        def _():
            robust_step(i, 0, carry, True)


def _attention(qn, qp, kn, kp, vt, seq):
    t = qn.shape[0]
    bsz = t // seq
    tq = min(TQ_ATTN, seq)
    nq = seq // tq
    q_spec = pl.BlockSpec((tq, LANES), lambda b, h, i: (b * nq + i, h))
    kn_spec = pl.BlockSpec((seq, LANES), lambda b, h, i: (b, h))
    kp_spec = pl.BlockSpec((seq, LANES), lambda b, h, i: (b, 0))
    vt_spec = pl.BlockSpec((LANES, seq), lambda b, h, i: (h, b))
    q_next_spec = pl.BlockSpec((tq, LANES), lambda b, h, i: (b * nq + jnp.minimum(i + 1, nq - 1), h))
    vmem = 2 * 3 * seq * LANES * 2 + 10 * tq * LANES * 2 + 3 * tq * tq * 4 + 2 * tq * tq * 2 + 8 * tq * LANES * 4 + (8 << 20)
    return pl.pallas_call(
        functools.partial(_attn_kernel, tq=tq),
        grid=(bsz, MLA_HEADS, nq),
        in_specs=[q_spec, q_spec, q_next_spec, q_next_spec, kn_spec, kp_spec, vt_spec],
        out_specs=q_spec,
        out_shape=jax.ShapeDtypeStruct((t, MLA_HEADS * MLA_V), BF16),
        scratch_shapes=[pltpu.VMEM((3, tq, tq), F32), pltpu.VMEM((3, SUBLANES, tq), F32),
                        pltpu.VMEM((2, tq, tq), BF16), pltpu.VMEM((MLA_V, tq), F32),
                        pltpu.VMEM((SUBLANES, tq), F32)],
        compiler_params=_params(("arbitrary", "arbitrary", "arbitrary"), vmem),
        name="mla_attention",
    )(qn, qp, qn, qp, kn, kp, vt)


def kernel(x, c, positions, mod_w, mod_b, ffn_in, ffn_out, ln_g, ln_b, even_w_in, conv_w, conv_b, rg_wa, rg_ba, rg_wx, rg_bx, rg_lam, gla_wa2, gla_ba, gla_norm_g, even_w_out, odd_w_in, q_norm_g, w_q_up, kv_norm_g, w_kv_up, odd_w_out):
    bsz, seq, d = x.shape
    depth = mod_w.shape[0]
    alpha = (2.0 * depth) ** 0.25
    mod = _mod_vectors(c, mod_w, mod_b)
    h = x.reshape(bsz * seq, d)
    for l in range(depth):
        mod_l = mod[l]
        h = _ffn_sublayer(h, mod_l, 0, ffn_in[l, 0], ffn_out[l, 0], ln_g[l, 0], ln_b[l, 0], alpha, seq)
        if l % 2 == 0:
            e = l // 2
            xa, gg, q, k, v, gs, la = _even_in(h, mod_l, 3, even_w_in[e], gla_wa2[e], gla_ba[e], seq)
            ya = _rglru(xa, gg, conv_w[e], conv_b[e], rg_wa[e], rg_ba[e], rg_wx[e], rg_bx[e], rg_lam[e], seq)
            yb = _gla(q, k, la, v, gs, gla_norm_g[e], seq)
            wo = even_w_out[e].astype(BF16)
            rgw = ya.shape[1]
            h = _proj_deepnorm(h, mod_l, 3, [ya, yb], [wo[:rgw], wo[rgw:]], ln_g[l, 1], ln_b[l, 1], alpha, seq)
        else:
            o = l // 2
            qn, qp, kn, kp, v = _odd_in(h, mod_l, 3, positions, odd_w_in[o], q_norm_g[o], w_q_up[o],
                                        kv_norm_g[o], w_kv_up[o], seq)
            att = _attention(qn, qp, kn, kp, v, seq)
            h = _proj_deepnorm(h, mod_l, 3, [att], [odd_w_out[o].astype(BF16)], ln_g[l, 1], ln_b[l, 1], alpha, seq)
        h = _ffn_sublayer(h, mod_l, 6, ffn_in[l, 1], ffn_out[l, 1], ln_g[l, 2], ln_b[l, 2], alpha, seq)
    return h.reshape(bsz, seq, d)
```
